```python
import jax, jax.numpy as jnp
from jax import lax
import numpy as np

D_MODEL = 1024
BATCH = 16
SEQ = 256
DEPTH = 2
DEC_BATCH = 8
DEC_SEQ = 4096
PAST_LEN = 512

GRID_W = 64
HEAD_DIM = 128
N_Q_HEADS = 8
N_KV_HEADS = 2
Q_PER_KV = N_Q_HEADS // N_KV_HEADS
ATTN_WIDTH = N_Q_HEADS * HEAD_DIM
KV_WIDTH = N_KV_HEADS * HEAD_DIM
Q_BLOCK = 128
ROPE_THETA = 10000.0
ROPE_PAIRS = HEAD_DIM // 4
POOL_WINDOWS = (2, 4, 8, 16)
N_POOL_GROUPS = 4
POOL_WIDTH = D_MODEL // 2
POOL_GROUP_DIM = POOL_WIDTH // N_POOL_GROUPS
CHUNK = 128
N_SGU_GROUPS = 4
SGU_WIDTH = D_MODEL // 2
SGU_GROUP_DIM = SGU_WIDTH // N_SGU_GROUPS
N_BRANCHES = 3
IN_WIDTH = ATTN_WIDTH + 2 * KV_WIDTH + POOL_WIDTH + 2 * SGU_WIDTH + N_BRANCHES * D_MODEL
SPLIT_Q = ATTN_WIDTH
SPLIT_K = SPLIT_Q + KV_WIDTH
SPLIT_V = SPLIT_K + KV_WIDTH
SPLIT_POOL = SPLIT_V + POOL_WIDTH
SPLIT_U = SPLIT_POOL + SGU_WIDTH
SPLIT_SV = SPLIT_U + SGU_WIDTH
D_FF = ((8 * D_MODEL + 3 * 256 - 1) // (3 * 256)) * 256
DEEPNORM_ALPHA = (2 * DEPTH) ** 0.25
DEEPNORM_BETA = (8 * DEPTH) ** -0.25
EPS = 1e-6

kernel_name = "hybrid_dit_gated_attn_pool_sgu_step"


def layer_norm(x, g, b):
    xf = x.astype(jnp.float32)
    mu = jnp.mean(xf, axis=-1, keepdims=True)
    xc = xf - mu
    var = jnp.mean(xc * xc, axis=-1, keepdims=True)
    y = xc * lax.rsqrt(var + EPS) * g.astype(jnp.float32) + b.astype(jnp.float32)
    return y.astype(x.dtype)


def plain_layer_norm(x):
    xf = x.astype(jnp.float32)
    mu = jnp.mean(xf, axis=-1, keepdims=True)
    xc = xf - mu
    var = jnp.mean(xc * xc, axis=-1, keepdims=True)
    return (xc * lax.rsqrt(var + EPS)).astype(x.dtype)


def rms_norm(x, g):
    xf = x.astype(jnp.float32)
    y = xf * lax.rsqrt(jnp.mean(xf * xf, axis=-1, keepdims=True) + EPS) * g.astype(jnp.float32)
    return y.astype(x.dtype)


def axial_rope_tables(rows):
    row = jnp.repeat(jnp.arange(rows), GRID_W).astype(jnp.float32)
    col = jnp.tile(jnp.arange(GRID_W), rows).astype(jnp.float32)
    inv_freq = ROPE_THETA ** (-jnp.arange(ROPE_PAIRS, dtype=jnp.float32) / ROPE_PAIRS)
    ang = jnp.stack([row[:, None] * inv_freq, col[:, None] * inv_freq], axis=1)
    return jnp.cos(ang), jnp.sin(ang)


def apply_axial_rope(x, cos, sin):
    B, S, H, _ = x.shape
    xr = x.astype(jnp.float32).reshape(B, S, H, 2, 2, ROPE_PAIRS)
    x1 = xr[..., 0, :]
    x2 = xr[..., 1, :]
    c = cos[None, :, None]
    s = sin[None, :, None]
    out = jnp.stack([x1 * c - x2 * s, x2 * c + x1 * s], axis=-2)
    return out.reshape(x.shape).astype(x.dtype)


def blocked_attention(q, k, v):
    B, S = q.shape[0], q.shape[1]
    nb = S // Q_BLOCK
    qb = q.reshape(B, nb, Q_BLOCK, N_KV_HEADS, Q_PER_KV, HEAD_DIM).transpose(1, 0, 2, 3, 4, 5)
    scale = HEAD_DIM ** -0.5

    def one_block(qblk):
        s = jnp.einsum('bqhgd,bkhd->bhgqk', qblk, k).astype(jnp.float32) * scale
        p = jax.nn.softmax(s, axis=-1)
        return jnp.einsum('bhgqk,bkhd->bqhgd', p.astype(v.dtype), v)

    out = lax.map(one_block, qb)
    return out.transpose(1, 0, 2, 3, 4, 5).reshape(B, S, ATTN_WIDTH)


def pool_mixer(xp, w_pg, pscale):
    B, S, _ = xp.shape
    xf = xp.astype(jnp.float32).reshape(B, S, N_POOL_GROUPS, POOL_GROUP_DIM)
    cs = jnp.concatenate([jnp.zeros((B, 1, N_POOL_GROUPS, POOL_GROUP_DIM), jnp.float32),
                          jnp.cumsum(xf, axis=1)], axis=1)
    w = jnp.array(POOL_WINDOWS, dtype=jnp.int32)[None, :]
    t = jnp.arange(S, dtype=jnp.int32)[:, None]
    lo = jnp.maximum(t - w // 2, 0)
    hi = jnp.minimum(t + (w - w // 2), S)
    g = jnp.arange(N_POOL_GROUPS, dtype=jnp.int32)[None, :]
    sums = cs[:, hi, g] - cs[:, lo, g]
    cnt = (hi - lo).astype(jnp.float32)[None, :, :, None]
    pooled = (sums / cnt - xf).astype(xp.dtype)
    y = jnp.einsum('bsgc,gcd->bsgd', pooled, w_pg).reshape(B, S, POOL_WIDTH)
    return y * pscale


def spatial_gating(u, v, w_s, b_s):
    B, S, _ = u.shape
    n = S // CHUNK
    vr = plain_layer_norm(v).reshape(B, n, CHUNK, N_SGU_GROUPS, SGU_GROUP_DIM)
    mixed = jnp.einsum('gpq,bnqgc->bnpgc', w_s, vr) + b_s.T[None, None, :, :, None]
    return u * mixed.reshape(B, S, SGU_WIDTH)


def trunk_layer(x, mod, rope, ctx_k, ctx_v, p):
    B, S, _ = x.shape
    sh1, sc1, g1, sh2, sc2, g2 = jnp.split(mod, 6, axis=-1)
    h = x * (1 + sc1) + sh1
    proj = h @ p['w_in']
    q, k, v, xp, xu, xv, gates = jnp.split(
        proj, [SPLIT_Q, SPLIT_K, SPLIT_V, SPLIT_POOL, SPLIT_U, SPLIT_SV], axis=-1)
    q = rms_norm(q.reshape(B, S, N_Q_HEADS, HEAD_DIM), p['q_norm_g'])
    k = rms_norm(k.reshape(B, S, N_KV_HEADS, HEAD_DIM), p['k_norm_g'])
    v = v.reshape(B, S, N_KV_HEADS, HEAD_DIM)
    if rope is None:
        k_all, v_all = k, v
    else:
        cos, sin = rope
        q = apply_axial_rope(q, cos, sin)
        k_lat = apply_axial_rope(k, cos, sin)
        k_all = jnp.concatenate([ctx_k, k_lat], axis=1)
        v_all = jnp.concatenate([ctx_v, v], axis=1)
    attn = blocked_attention(q, k_all, v_all)
    pool = pool_mixer(xp, p['w_pool_g'], p['pool_scale'])
    sgu = spatial_gating(jax.nn.gelu(xu, approximate=False), jax.nn.gelu(xv, approximate=False),
                         p['w_sgu'], p['b_sgu'])
    ga, gp, gs = jnp.split(jax.nn.sigmoid(gates), N_BRANCHES, axis=-1)
    merged = ga * (attn @ p['w_attn_o']) + gp * (pool @ p['w_pool_o']) + gs * (sgu @ p['w_sgu_o'])
    mix = merged @ p['w_out']
    x = layer_norm(DEEPNORM_ALPHA * x + g1 * mix, p['ln1_g'], p['ln1_b'])
    h2 = x * (1 + sc2) + sh2
    a, b = jnp.split(h2 @ p['w_ffn_in'], 2, axis=-1)
    f = (jax.nn.silu(a) * b) @ p['w_ffn_out']
    x = layer_norm(DEEPNORM_ALPHA * x + g2 * f, p['ln2_g'], p['ln2_b'])
    return x, k, v


def setup_inputs(seed: int = 0) -> dict:
    key = jax.random.key(seed)
    ks = jax.random.split(key, 25)
    nrm = jax.random.normal
    D = D_MODEL
    return {
        "x_prompt": nrm(ks[0], (BATCH, SEQ, D), jnp.float32),
        "x_sample": nrm(ks[1], (DEC_BATCH, DEC_SEQ, D), jnp.float32),
        "cache_k": nrm(ks[2], (DEC_BATCH, DEPTH, PAST_LEN, N_KV_HEADS, HEAD_DIM), jnp.float32),
        "cache_v": nrm(ks[3], (DEC_BATCH, DEPTH, PAST_LEN, N_KV_HEADS, HEAD_DIM), jnp.float32),
        "c": nrm(ks[4], (DEC_BATCH, D), jnp.float32),
        "c_ctx": nrm(ks[5], (D,), jnp.float32),
        "w_ada": nrm(ks[6], (DEPTH, D, 6 * D), jnp.float32) * (0.5 * D ** -0.5),
        "b_ada": nrm(ks[7], (DEPTH, 6 * D), jnp.float32) * 0.02,
        "w_in": nrm(ks[8], (DEPTH, D, IN_WIDTH), jnp.float32) * D ** -0.5,
        "q_norm_g": 1.0 + 0.02 * nrm(ks[9], (DEPTH, HEAD_DIM), jnp.float32),
        "k_norm_g": 1.0 + 0.02 * nrm(ks[10], (DEPTH, HEAD_DIM), jnp.float32),
        "w_pool_g": nrm(ks[11], (DEPTH, N_POOL_GROUPS, POOL_GROUP_DIM, POOL_GROUP_DIM), jnp.float32) * POOL_GROUP_DIM ** -0.5,
        "pool_scale": 1.0 + 0.1 * nrm(ks[12], (DEPTH, POOL_WIDTH), jnp.float32),
        "w_sgu": nrm(ks[13], (DEPTH, N_SGU_GROUPS, CHUNK, CHUNK), jnp.float32) * CHUNK ** -0.5,
        "b_sgu": 1.0 + 0.1 * nrm(ks[14], (DEPTH, N_SGU_GROUPS, CHUNK), jnp.float32),
        "w_attn_o": nrm(ks[15], (DEPTH, ATTN_WIDTH, D), jnp.float32) * (ATTN_WIDTH ** -0.5 * DEEPNORM_BETA),
        "w_pool_o": nrm(ks[16], (DEPTH, POOL_WIDTH, D), jnp.float32) * (POOL_WIDTH ** -0.5 * DEEPNORM_BETA),
        "w_sgu_o": nrm(ks[17], (DEPTH, SGU_WIDTH, D), jnp.float32) * (SGU_WIDTH ** -0.5 * DEEPNORM_BETA),
        "w_out": nrm(ks[18], (DEPTH, D, D), jnp.float32) * (D ** -0.5 * DEEPNORM_BETA),
        "ln1_g": 1.0 + 0.02 * nrm(ks[19], (DEPTH, D), jnp.float32),
        "ln1_b": 0.02 * nrm(ks[20], (DEPTH, D), jnp.float32),
        "w_ffn_in": nrm(ks[21], (DEPTH, D, 2 * D_FF), jnp.float32) * D ** -0.5,
        "w_ffn_out": nrm(ks[22], (DEPTH, D_FF, D), jnp.float32) * (D_FF ** -0.5 * DEEPNORM_BETA),
        "ln2_g": 1.0 + 0.02 * nrm(ks[23], (DEPTH, D), jnp.float32),
        "ln2_b": 0.02 * nrm(ks[24], (DEPTH, D), jnp.float32),
    }


def reference(x_prompt, x_sample, cache_k, cache_v, c, c_ctx, w_ada, b_ada, w_in, q_norm_g,
              k_norm_g, w_pool_g, pool_scale, w_sgu, b_sgu, w_attn_o, w_pool_o, w_sgu_o, w_out,
              ln1_g, ln1_b, w_ffn_in, w_ffn_out, ln2_g, ln2_b):
    rows = x_sample.shape[1] // GRID_W
    rope = axial_rope_tables(rows)
    silu_ctx = jax.nn.silu(c_ctx)
    silu_c = jax.nn.silu(c)
    y_p = x_prompt
    y_s = x_sample
    new_k = []
    new_v = []
    for l in range(DEPTH):
        p = {
            'w_in': w_in[l], 'q_norm_g': q_norm_g[l], 'k_norm_g': k_norm_g[l],
            'w_pool_g': w_pool_g[l], 'pool_scale': pool_scale[l],
            'w_sgu': w_sgu[l], 'b_sgu': b_sgu[l],
            'w_attn_o': w_attn_o[l], 'w_pool_o': w_pool_o[l], 'w_sgu_o': w_sgu_o[l],
            'w_out': w_out[l], 'ln1_g': ln1_g[l], 'ln1_b': ln1_b[l],
            'w_ffn_in': w_ffn_in[l], 'w_ffn_out': w_ffn_out[l],
            'ln2_g': ln2_g[l], 'ln2_b': ln2_b[l],
        }
        mod_ctx = (silu_ctx @ w_ada[l] + b_ada[l])[None, None, :]
        mod_lat = (silu_c @ w_ada[l] + b_ada[l])[:, None, :]
        y_p, k_ctx, v_ctx = trunk_layer(y_p, mod_ctx, None, None, None, p)
        new_k.append(k_ctx)
        new_v.append(v_ctx)
        y_s, _, _ = trunk_layer(y_s, mod_lat, rope, cache_k[:, l], cache_v[:, l], p)
    new_cache_k = jnp.stack(new_k, axis=1)
    new_cache_v = jnp.stack(new_v, axis=1)
    return (y_p, y_s, new_cache_k, new_cache_v)
```

```python
import functools
import math

import jax
import jax.numpy as jnp
from jax import lax
from jax.experimental import pallas as pl
from jax.experimental.pallas import tpu as pltpu

D_MODEL = 1024
DEPTH = 2
GRID_W = 64
HEAD_DIM = 128
N_Q_HEADS = 8
N_KV_HEADS = 2
Q_PER_KV = N_Q_HEADS // N_KV_HEADS
ATTN_WIDTH = N_Q_HEADS * HEAD_DIM
KV_WIDTH = N_KV_HEADS * HEAD_DIM
ROPE_THETA = 10000.0
ROPE_PAIRS = HEAD_DIM // 4
POOL_WINDOWS = (2, 4, 8, 16)
N_POOL_GROUPS = 4
POOL_WIDTH = D_MODEL // 2
POOL_GROUP_DIM = POOL_WIDTH // N_POOL_GROUPS
POOL_HALO = max(POOL_WINDOWS) // 2
CHUNK = 128
N_SGU_GROUPS = 4
SGU_WIDTH = D_MODEL // 2
SGU_GROUP_DIM = SGU_WIDTH // N_SGU_GROUPS
N_BRANCHES = 3
IN_WIDTH = ATTN_WIDTH + 2 * KV_WIDTH + POOL_WIDTH + 2 * SGU_WIDTH + N_BRANCHES * D_MODEL
SPLIT_Q = ATTN_WIDTH
SPLIT_K = SPLIT_Q + KV_WIDTH
SPLIT_V = SPLIT_K + KV_WIDTH
SPLIT_POOL = SPLIT_V + POOL_WIDTH
SPLIT_U = SPLIT_POOL + SGU_WIDTH
SPLIT_SV = SPLIT_U + SGU_WIDTH
D_FF = ((8 * D_MODEL + 3 * 256 - 1) // (3 * 256)) * 256
DEEPNORM_ALPHA = (2 * DEPTH) ** 0.25
EPS = 1e-6

Q_PRESCALE = HEAD_DIM ** -0.5 * math.log2(math.e)
SQRT_HALF = 0.5 ** 0.5

V7X_VMEM_LIMIT_BYTES = 56 * 1024 * 1024
ADA_ROWS = 16
ADA_TN = 1536
TOKEN_TILE = 512
ATTN_Q_TILE = 256
ATTN_KV_CHUNK = 512
FFN_CHUNKS = 2

BF16 = jnp.bfloat16
F32 = jnp.float32


def _dot(a, b):
    return jnp.dot(a, b, preferred_element_type=F32)


def _resident(shape):
    nd = len(shape)
    return pl.BlockSpec(shape, lambda *_: (0,) * nd, pipeline_mode=pl.Buffered(1))


def _params(n_grid):
    return pltpu.CompilerParams(dimension_semantics=("arbitrary",) * n_grid,
                                vmem_limit_bytes=V7X_VMEM_LIMIT_BYTES)


def _layer_norm(y, g, b):
    mu = jnp.mean(y, axis=-1, keepdims=True)
    yc = y - mu
    var = jnp.mean(yc * yc, axis=-1, keepdims=True)
    return yc * lax.rsqrt(var + EPS) * g + b


def _gelu(x):
    return 0.5 * x * (1.0 + lax.erf(x * SQRT_HALF))


def _ada_kernel(c_ref, w_ref, b_ref, o_ref):
    c = c_ref[...]
    s = c * jax.nn.sigmoid(c)
    o_ref[0] = _dot(s.astype(BF16), w_ref[0].astype(BF16)) + b_ref[0]


def _ada_rows(cvec, w_ada, b_ada):
    n = 6 * D_MODEL
    return pl.pallas_call(
        _ada_kernel,
        grid=(DEPTH, n // ADA_TN),
        in_specs=[pl.BlockSpec((ADA_ROWS, D_MODEL), lambda l, j: (0, 0)),
                  pl.BlockSpec((1, D_MODEL, ADA_TN), lambda l, j: (l, 0, j)),
                  pl.BlockSpec((1, 1, ADA_TN), lambda l, j: (l, 0, j))],
        out_specs=pl.BlockSpec((1, ADA_ROWS, ADA_TN), lambda l, j: (l, 0, j)),
        out_shape=jax.ShapeDtypeStruct((DEPTH, ADA_ROWS, n), F32),
        compiler_params=_params(2),
        name="ada_rows",
    )(cvec, w_ada, b_ada.reshape(DEPTH, 1, n))


def _inproj_kernel(rope, emit_cache, x_ref, mod_ref, w_ref, qg_ref, kg_ref, *refs):
    refs = list(refs)
    if rope:
        cos_ref, sina_ref, sinb_ref = refs[:3]
        refs = refs[3:]
    q_ref, k_ref, vt_ref, xp_ref, u_ref, vn_ref, g_ref = refs[:7]
    if emit_cache:
        kc_ref, vc_ref = refs[7:9]

    x = x_ref[0]
    mod = mod_ref[0]
    sh1 = mod[:, 0:D_MODEL]
    sc1 = mod[:, D_MODEL:2 * D_MODEL]
    h = (x * (1.0 + sc1) + sh1).astype(BF16)

    def proj(lo, hi):
        return _dot(h, w_ref[:, lo:hi])

    def rms(t, g):
        return t * lax.rsqrt(jnp.mean(t * t, axis=-1, keepdims=True) + EPS) * g

    def rotate(t):
        return (t * cos_ref[...] + pltpu.roll(t, HEAD_DIM - ROPE_PAIRS, 1) * sina_ref[...]
                + pltpu.roll(t, ROPE_PAIRS, 1) * sinb_ref[...])

    qp = proj(0, SPLIT_Q)
    qg = qg_ref[...]
    for hd in range(N_Q_HEADS):
        sl = slice(hd * HEAD_DIM, (hd + 1) * HEAD_DIM)
        t = rms(qp[:, sl], qg)
        if rope:
            t = rotate(t)
        q_ref[0, :, sl] = (t * Q_PRESCALE).astype(BF16)

    kv = proj(SPLIT_Q, SPLIT_V)
    kg = kg_ref[...]
    for hd in range(N_KV_HEADS):
        sl = slice(hd * HEAD_DIM, (hd + 1) * HEAD_DIM)
        t = rms(kv[:, sl], kg)
        if emit_cache:
            kc_ref[0, :, sl] = t
        if rope:
            t = rotate(t)
        k_ref[0, :, sl] = t.astype(BF16)
    v = kv[:, KV_WIDTH:]
    if emit_cache:
        vc_ref[0] = v
    vt_ref[0] = v.T.astype(BF16)

    xp_ref[0] = proj(SPLIT_V, SPLIT_POOL)

    u_ref[0] = _gelu(proj(SPLIT_POOL, SPLIT_U)).astype(BF16)
    gv = _gelu(proj(SPLIT_U, SPLIT_SV))
    mu = jnp.mean(gv, axis=-1, keepdims=True)
    gc = gv - mu
    var = jnp.mean(gc * gc, axis=-1, keepdims=True)
    vn_ref[0] = (gc * lax.rsqrt(var + EPS)).astype(BF16)

    for br in range(N_BRANCHES):
        lo = SPLIT_SV + br * D_MODEL
        g_ref[0, :, br * D_MODEL:(br + 1) * D_MODEL] = jax.nn.sigmoid(proj(lo, lo + D_MODEL)).astype(BF16)


def _inproj(x, mod, w_in, qg, kg, rope_tabs, tm, emit_cache):
    b, s, _ = x.shape
    rope = rope_tabs is not None
    per_batch_mod = mod.shape[0] > 1
    tok = lambda w: pl.BlockSpec((1, tm, w), lambda bi, i: (bi, i, 0))
    in_specs = [tok(D_MODEL),
                pl.BlockSpec((1, 1, 6 * D_MODEL), (lambda bi, i: (bi, 0, 0)) if per_batch_mod else (lambda bi, i: (0, 0, 0))),
                _resident((D_MODEL, IN_WIDTH)), _resident((1, HEAD_DIM)), _resident((1, HEAD_DIM))]
    args = [x, mod, w_in, qg, kg]
    if rope:
        in_specs += [pl.BlockSpec((tm, HEAD_DIM), lambda bi, i: (i, 0))] * 3
        args += list(rope_tabs)
    out_specs = [tok(ATTN_WIDTH), tok(KV_WIDTH), pl.BlockSpec((1, KV_WIDTH, tm), lambda bi, i: (bi, 0, i)),
                 tok(POOL_WIDTH), tok(SGU_WIDTH), tok(SGU_WIDTH), tok(N_BRANCHES * D_MODEL)]
    sds = lambda w, dt: jax.ShapeDtypeStruct((b, s, w), dt)
    out_shape = [sds(ATTN_WIDTH, BF16), sds(KV_WIDTH, BF16), jax.ShapeDtypeStruct((b, KV_WIDTH, s), BF16),
                 sds(POOL_WIDTH, F32), sds(SGU_WIDTH, BF16), sds(SGU_WIDTH, BF16), sds(N_BRANCHES * D_MODEL, BF16)]
    if emit_cache:
        out_specs += [tok(KV_WIDTH), tok(KV_WIDTH)]
        out_shape += [sds(KV_WIDTH, F32), sds(KV_WIDTH, F32)]
    return pl.pallas_call(
        functools.partial(_inproj_kernel, rope, emit_cache),
        grid=(b, s // tm), in_specs=in_specs, out_specs=out_specs, out_shape=out_shape,
        compiler_params=_params(2), name="inproj_rope" if rope else "inproj_ctx",
    )(*args)


def _attn_kernel(n_ctx, n_chunks, tkc, q_ref, k_ref, vt_ref, *refs):
    refs = list(refs)
    if n_ctx:
        ck_ref, cv_ref = refs[:2]
        refs = refs[2:]
    o_ref, kall, vtall, m_scr, l_scr, acc_scr = refs
    tq = q_ref.shape[1]

    @pl.when(pl.program_id(1) == 0)
    def _stage_keys():
        if n_ctx:
            kall[0:n_ctx, :] = ck_ref[0].astype(BF16)
            cvt = cv_ref[0].T.astype(BF16)
            for c in range(n_ctx // tkc):
                vtall[c] = cvt[:, c * tkc:(c + 1) * tkc]
        kall[n_ctx:, :] = k_ref[0]
        for c in range(n_ctx // tkc, n_chunks):
            lo = c * tkc - n_ctx
            vtall[c] = vt_ref[0, :, lo:lo + tkc]

    m_scr[...] = jnp.full(m_scr.shape, -jnp.inf, F32)
    l_scr[...] = jnp.zeros(l_scr.shape, F32)
    acc_scr[...] = jnp.zeros(acc_scr.shape, F32)

    def chunk_step(c, carry):
        kc = kall[pl.ds(pl.multiple_of(c * tkc, tkc), tkc), :]
        vtc = vtall[c]
        for hd in range(N_Q_HEADS):
            g = hd // Q_PER_KV
            gl = slice(g * HEAD_DIM, (g + 1) * HEAD_DIM)
            qh = q_ref[0, :, hd * HEAD_DIM:(hd + 1) * HEAD_DIM]
            st = lax.dot_general(kc[:, gl], qh, (((1,), (1,)), ((), ())), preferred_element_type=F32)
            m_old = m_scr[hd]
            m_new = jnp.maximum(m_old, jnp.max(st, axis=0, keepdims=True))
            p = jnp.exp2(st - m_new)
            alpha = jnp.exp2(m_old - m_new)
            l_scr[hd] = alpha * l_scr[hd] + jnp.sum(p, axis=0, keepdims=True)
            acc_scr[hd] = alpha * acc_scr[hd] + _dot(vtc[gl, :], p.astype(BF16))
            m_scr[hd] = m_new
        return carry

    lax.fori_loop(0, n_chunks, chunk_step, 0)

    for hd in range(N_Q_HEADS):
        ot = acc_scr[hd] / l_scr[hd]
        o_ref[0, :, hd * HEAD_DIM:(hd + 1) * HEAD_DIM] = ot.T.astype(BF16)


def _attention(q, k, vt, ctx_k, ctx_v, tq, tkc):
    b, s, _ = q.shape
    n_ctx = 0 if ctx_k is None else ctx_k.shape[1]
    total = n_ctx + s
    n_chunks = total // tkc
    in_specs = [pl.BlockSpec((1, tq, ATTN_WIDTH), lambda bi, i: (bi, i, 0)),
                pl.BlockSpec((1, s, KV_WIDTH), lambda bi, i: (bi, 0, 0)),
                pl.BlockSpec((1, KV_WIDTH, s), lambda bi, i: (bi, 0, 0))]
    args = [q, k, vt]
    if n_ctx:
        in_specs += [pl.BlockSpec((1, n_ctx, KV_WIDTH), lambda bi, i: (bi, 0, 0))] * 2
        args += [ctx_k, ctx_v]
    return pl.pallas_call(
        functools.partial(_attn_kernel, n_ctx, n_chunks, tkc),
        grid=(b, s // tq), in_specs=in_specs,
        out_specs=pl.BlockSpec((1, tq, ATTN_WIDTH), lambda bi, i: (bi, i, 0)),
        out_shape=jax.ShapeDtypeStruct((b, s, ATTN_WIDTH), BF16),
        scratch_shapes=[pltpu.VMEM((total, KV_WIDTH), BF16),
                        pltpu.VMEM((n_chunks, KV_WIDTH, tkc), BF16),
                        pltpu.VMEM((N_Q_HEADS, 1, tq), F32),
                        pltpu.VMEM((N_Q_HEADS, 1, tq), F32),
                        pltpu.VMEM((N_Q_HEADS, HEAD_DIM, tq), F32)],
        compiler_params=_params(2), name="attn_lat" if n_ctx else "attn_ctx",
    )(*args)


def _mix_kernel(seq, x_ref, mod_ref, attn_ref, xp_ref, xprev_ref, xnext_ref, u_ref, vn_ref, g_ref,
                wao_ref, wpo_ref, wso_ref, wout_ref, wpg_ref, psc_ref, ws_ref, bs_ref, lng_ref, lnb_ref,
                o_ref, xe_scr, br_scr):
    tm = x_ref.shape[1]
    i = pl.program_id(1)
    nt = pl.num_programs(1)

    xe_scr[0:POOL_HALO, :] = jnp.where(i > 0, xprev_ref[0], 0.0)
    xe_scr[POOL_HALO:POOL_HALO + tm, :] = xp_ref[0]
    xe_scr[POOL_HALO + tm:, :] = jnp.where(i < nt - 1, xnext_ref[0], 0.0)
    pos = i * tm + lax.broadcasted_iota(jnp.int32, (tm, 1), 0)
    for g, w in enumerate(POOL_WINDOWS):
        gl = slice(g * POOL_GROUP_DIM, (g + 1) * POOL_GROUP_DIM)
        sums = xe_scr[pl.ds(POOL_HALO - w // 2, tm), gl]
        for d in range(1 - w // 2, w - w // 2):
            sums = sums + xe_scr[pl.ds(POOL_HALO + d, tm), gl]
        cnt = (jnp.minimum(pos + (w - w // 2), seq) - jnp.maximum(pos - w // 2, 0)).astype(F32)
        pooled = (sums / cnt - xp_ref[0, :, gl]).astype(BF16)
        br_scr[:, gl] = (_dot(pooled, wpg_ref[g]) * psc_ref[:, gl]).astype(BF16)
    gates = g_ref[0]
    merged = gates[:, D_MODEL:2 * D_MODEL].astype(F32) * _dot(br_scr[...], wpo_ref[...])

    for c in range(tm // CHUNK):
        rows = slice(c * CHUNK, (c + 1) * CHUNK)
        for g in range(N_SGU_GROUPS):
            gl = slice(g * SGU_GROUP_DIM, (g + 1) * SGU_GROUP_DIM)
            mixed = _dot(ws_ref[g], vn_ref[0, rows, gl]) + bs_ref[:, g:g + 1]
            br_scr[rows, gl] = (u_ref[0, rows, gl].astype(F32) * mixed).astype(BF16)
    merged = merged + gates[:, 2 * D_MODEL:].astype(F32) * _dot(br_scr[...], wso_ref[...])

    merged = merged + gates[:, 0:D_MODEL].astype(F32) * _dot(attn_ref[0], wao_ref[...])
    mix = _dot(merged.astype(BF16), wout_ref[...])
    g1 = mod_ref[0][:, 2 * D_MODEL:3 * D_MODEL]
    o_ref[0] = _layer_norm(DEEPNORM_ALPHA * x_ref[0] + g1 * mix, lng_ref[...], lnb_ref[...])


def _mix(x, mod, attn, xp, u, vn, gates, w, tm):
    b, s, _ = x.shape
    per_batch_mod = mod.shape[0] > 1
    hb = tm // POOL_HALO
    last_halo = s // POOL_HALO - 1
    tok = lambda wd: pl.BlockSpec((1, tm, wd), lambda bi, i: (bi, i, 0))
    in_specs = [tok(D_MODEL),
                pl.BlockSpec((1, 1, 6 * D_MODEL), (lambda bi, i: (bi, 0, 0)) if per_batch_mod else (lambda bi, i: (0, 0, 0))),
                tok(ATTN_WIDTH), tok(POOL_WIDTH),
                pl.BlockSpec((1, POOL_HALO, POOL_WIDTH), lambda bi, i: (bi, jnp.maximum(i * hb - 1, 0), 0)),
                pl.BlockSpec((1, POOL_HALO, POOL_WIDTH), lambda bi, i: (bi, jnp.minimum((i + 1) * hb, last_halo), 0)),
                tok(SGU_WIDTH), tok(SGU_WIDTH), tok(N_BRANCHES * D_MODEL),
                _resident((ATTN_WIDTH, D_MODEL)), _resident((POOL_WIDTH, D_MODEL)), _resident((SGU_WIDTH, D_MODEL)),
                _resident((D_MODEL, D_MODEL)), _resident((N_POOL_GROUPS, POOL_GROUP_DIM, POOL_GROUP_DIM)),
                _resident((1, POOL_WIDTH)), _resident((N_SGU_GROUPS, CHUNK, CHUNK)), _resident((CHUNK, N_SGU_GROUPS)),
                _resident((1, D_MODEL)), _resident((1, D_MODEL))]
    return pl.pallas_call(
        functools.partial(_mix_kernel, s),
        grid=(b, s // tm), in_specs=in_specs, out_specs=tok(D_MODEL),
        out_shape=jax.ShapeDtypeStruct((b, s, D_MODEL), F32),
        scratch_shapes=[pltpu.VMEM((tm + 2 * POOL_HALO, POOL_WIDTH), F32), pltpu.VMEM((tm, POOL_WIDTH), BF16)],
        compiler_params=_params(2), name="mix",
    )(x, mod, attn, xp, xp, xp, u, vn, gates, w["attn_o"], w["pool_o"], w["sgu_o"], w["out"], w["pool_g"],
      w["pool_scale"], w["sgu"], w["b_sgu"], w["ln1_g"], w["ln1_b"])


def _ffn_kernel(x_ref, mod_ref, wa_ref, wb_ref, wo_ref, lng_ref, lnb_ref, o_ref):
    x = x_ref[0]
    mod = mod_ref[0]
    sh2 = mod[:, 3 * D_MODEL:4 * D_MODEL]
    sc2 = mod[:, 4 * D_MODEL:5 * D_MODEL]
    g2 = mod[:, 5 * D_MODEL:6 * D_MODEL]
    h = (x * (1.0 + sc2) + sh2).astype(BF16)
    step = D_FF // FFN_CHUNKS
    f = None
    for j in range(FFN_CHUNKS):
        sl = slice(j * step, (j + 1) * step)
        a = _dot(h, wa_ref[:, sl])
        bgate = _dot(h, wb_ref[:, sl])
        part = _dot((a * jax.nn.sigmoid(a) * bgate).astype(BF16), wo_ref[sl, :])
        f = part if f is None else f + part
    o_ref[0] = _layer_norm(DEEPNORM_ALPHA * x + g2 * f, lng_ref[...], lnb_ref[...])


def _ffn(x, mod, w, tm):
    b, s, _ = x.shape
    per_batch_mod = mod.shape[0] > 1
    tok = pl.BlockSpec((1, tm, D_MODEL), lambda bi, i: (bi, i, 0))
    in_specs = [tok,
                pl.BlockSpec((1, 1, 6 * D_MODEL), (lambda bi, i: (bi, 0, 0)) if per_batch_mod else (lambda bi, i: (0, 0, 0))),
                _resident((D_MODEL, D_FF)), _resident((D_MODEL, D_FF)), _resident((D_FF, D_MODEL)),
                _resident((1, D_MODEL)), _resident((1, D_MODEL))]
    return pl.pallas_call(
        _ffn_kernel, grid=(b, s // tm), in_specs=in_specs, out_specs=tok,
        out_shape=jax.ShapeDtypeStruct((b, s, D_MODEL), F32),
        compiler_params=_params(2), name="ffn",
    )(x, mod, w["ffn_a"], w["ffn_b"], w["ffn_o"], w["ln2_g"], w["ln2_b"])


def _rope_tables(seq):
    rows = seq // GRID_W
    row = jnp.repeat(jnp.arange(rows), GRID_W).astype(F32)
    col = jnp.tile(jnp.arange(GRID_W), rows).astype(F32)
    inv_freq = ROPE_THETA ** (-jnp.arange(ROPE_PAIRS, dtype=F32) / ROPE_PAIRS)
    ar = row[:, None] * inv_freq
    ac = col[:, None] * inv_freq
    z = jnp.zeros_like(ar)
    cos = jnp.concatenate([jnp.cos(ar), jnp.cos(ar), jnp.cos(ac), jnp.cos(ac)], axis=1)
    sina = jnp.concatenate([-jnp.sin(ar), z, -jnp.sin(ac), z], axis=1)
    sinb = jnp.concatenate([z, jnp.sin(ar), z, jnp.sin(ac)], axis=1)
    return cos, sina, sinb


def _trunk_layer(x, mod, rope_tabs, ctx_k, ctx_v, w, tm, emit_cache):
    outs = _inproj(x, mod, w["in"], w["q_norm_g"], w["k_norm_g"], rope_tabs, tm, emit_cache)
    q, k, vt, xp, u, vn, gates = outs[:7]
    s = x.shape[1]
    attn = _attention(q, k, vt, ctx_k, ctx_v, min(ATTN_Q_TILE, s), min(ATTN_KV_CHUNK, s))
    x = _mix(x, mod, attn, xp, u, vn, gates, w, tm)
    x = _ffn(x, mod, w, tm)
    return x, outs[7:]


def kernel(x_prompt, x_sample, cache_k, cache_v, c, c_ctx, w_ada, b_ada, w_in, q_norm_g, k_norm_g, w_pool_g,
           pool_scale, w_sgu, b_sgu, w_attn_o, w_pool_o, w_sgu_o, w_out, ln1_g, ln1_b, w_ffn_in, w_ffn_out,
           ln2_g, ln2_b):
    n_dec = c.shape[0]
    cvec = jnp.concatenate([c, c_ctx[None, :], jnp.zeros((ADA_ROWS - n_dec - 1, D_MODEL), F32)], axis=0)
    mods = _ada_rows(cvec, w_ada, b_ada)
    rope_tabs = _rope_tables(x_sample.shape[1])
    n_past = cache_k.shape[2]
    y_p, y_s = x_prompt, x_sample
    new_k, new_v = [], []
    for l in range(DEPTH):
        row = lambda a: a[l].reshape(1, -1)
        w = {
            "in": w_in[l].astype(BF16), "q_norm_g": row(q_norm_g), "k_norm_g": row(k_norm_g),
            "pool_g": w_pool_g[l].astype(BF16), "pool_scale": row(pool_scale),
            "sgu": w_sgu[l].astype(BF16), "b_sgu": b_sgu[l].T,
            "attn_o": w_attn_o[l].astype(BF16), "pool_o": w_pool_o[l].astype(BF16),
            "sgu_o": w_sgu_o[l].astype(BF16), "out": w_out[l].astype(BF16),
            "ln1_g": row(ln1_g), "ln1_b": row(ln1_b),
            "ffn_a": w_ffn_in[l, :, :D_FF].astype(BF16), "ffn_b": w_ffn_in[l, :, D_FF:].astype(BF16),
            "ffn_o": w_ffn_out[l].astype(BF16), "ln2_g": row(ln2_g), "ln2_b": row(ln2_b),
        }
        mod_lat = mods[l, :n_dec].reshape(n_dec, 1, 6 * D_MODEL)
        mod_ctx = mods[l, n_dec:n_dec + 1].reshape(1, 1, 6 * D_MODEL)
        y_p, (k_ctx, v_ctx) = _trunk_layer(y_p, mod_ctx, None, None, None, w, x_prompt.shape[1], True)
        new_k.append(k_ctx)
        new_v.append(v_ctx)
        ctx_k = cache_k[:, l].reshape(n_dec, n_past, KV_WIDTH)
        ctx_v = cache_v[:, l].reshape(n_dec, n_past, KV_WIDTH)
        y_s, _ = _trunk_layer(y_s, mod_lat, rope_tabs, ctx_k, ctx_v, w, TOKEN_TILE, False)
    cache_shape = (x_prompt.shape[0], DEPTH, x_prompt.shape[1], N_KV_HEADS, HEAD_DIM)
    new_cache_k = jnp.stack(new_k, axis=1).reshape(cache_shape)
    new_cache_v = jnp.stack(new_v, axis=1).reshape(cache_shape)
    return (y_p, y_s, new_cache_k, new_cache_v)
```

```python
import functools
import math

import jax
import jax.numpy as jnp
from jax import lax
from jax.experimental import pallas as pl
from jax.experimental.pallas import tpu as pltpu

D_MODEL = 1024
DEPTH = 2
GRID_W = 64
HEAD_DIM = 128
N_Q_HEADS = 8
N_KV_HEADS = 2
Q_PER_KV = N_Q_HEADS // N_KV_HEADS
ATTN_WIDTH = N_Q_HEADS * HEAD_DIM
KV_WIDTH = N_KV_HEADS * HEAD_DIM
ROPE_THETA = 10000.0
ROPE_PAIRS = HEAD_DIM // 4
POOL_WINDOWS = (2, 4, 8, 16)
N_POOL_GROUPS = 4
POOL_WIDTH = D_MODEL // 2
POOL_GROUP_DIM = POOL_WIDTH // N_POOL_GROUPS
POOL_HALO = max(POOL_WINDOWS) // 2
CHUNK = 128
N_SGU_GROUPS = 4
SGU_WIDTH = D_MODEL // 2
SGU_GROUP_DIM = SGU_WIDTH // N_SGU_GROUPS
N_BRANCHES = 3
IN_WIDTH = ATTN_WIDTH + 2 * KV_WIDTH + POOL_WIDTH + 2 * SGU_WIDTH + N_BRANCHES * D_MODEL
SPLIT_Q = ATTN_WIDTH
SPLIT_K = SPLIT_Q + KV_WIDTH
SPLIT_V = SPLIT_K + KV_WIDTH
SPLIT_POOL = SPLIT_V + POOL_WIDTH
SPLIT_U = SPLIT_POOL + SGU_WIDTH
SPLIT_SV = SPLIT_U + SGU_WIDTH
D_FF = ((8 * D_MODEL + 3 * 256 - 1) // (3 * 256)) * 256
DEEPNORM_ALPHA = (2 * DEPTH) ** 0.25
EPS = 1e-6

Q_PRESCALE = HEAD_DIM ** -0.5 * math.log2(math.e)
SQRT_HALF = 0.5 ** 0.5

V7X_VMEM_LIMIT_BYTES = 56 * 1024 * 1024
ADA_ROWS = 16
ADA_TN = 1536
TOKEN_TILE = 512
ATTN_Q_TILE = 256
ATTN_KV_CHUNK = 768
FFN_CHUNKS = 2

BF16 = jnp.bfloat16
F32 = jnp.float32


def _dot(a, b):
    return jnp.dot(a, b, preferred_element_type=F32)


def _resident(shape):
    nd = len(shape)
    return pl.BlockSpec(shape, lambda *_: (0,) * nd, pipeline_mode=pl.Buffered(1))


def _params(n_grid):
    return pltpu.CompilerParams(dimension_semantics=("arbitrary",) * n_grid,
                                vmem_limit_bytes=V7X_VMEM_LIMIT_BYTES)


def _layer_norm(y, g, b):
    mu = jnp.mean(y, axis=-1, keepdims=True)
    yc = y - mu
    var = jnp.mean(yc * yc, axis=-1, keepdims=True)
    return yc * lax.rsqrt(var + EPS) * g + b


def _gelu(x):
    return 0.5 * x * (1.0 + lax.erf(x * SQRT_HALF))


def _ada_kernel(c_ref, w_ref, b_ref, o_ref):
    c = c_ref[...]
    s = c * jax.nn.sigmoid(c)
    o_ref[0] = _dot(s.astype(BF16), w_ref[0].astype(BF16)) + b_ref[0]


def _ada_rows(cvec, w_ada, b_ada):
    n = 6 * D_MODEL
    return pl.pallas_call(
        _ada_kernel,
        grid=(DEPTH, n // ADA_TN),
        in_specs=[pl.BlockSpec((ADA_ROWS, D_MODEL), lambda l, j: (0, 0)),
                  pl.BlockSpec((1, D_MODEL, ADA_TN), lambda l, j: (l, 0, j)),
                  pl.BlockSpec((1, 1, ADA_TN), lambda l, j: (l, 0, j))],
        out_specs=pl.BlockSpec((1, ADA_ROWS, ADA_TN), lambda l, j: (l, 0, j)),
        out_shape=jax.ShapeDtypeStruct((DEPTH, ADA_ROWS, n), F32),
        compiler_params=_params(2),
        name="ada_rows",
    )(cvec, w_ada, b_ada.reshape(DEPTH, 1, n))


def _inproj_kernel(rope, emit_cache, x_ref, mod_ref, w_ref, qg_ref, kg_ref, *refs):
    refs = list(refs)
    if rope:
        cos_ref, sina_ref, sinb_ref = refs[:3]
        refs = refs[3:]
    q_ref, k_ref, vt_ref, xp_ref, u_ref, vn_ref, g_ref = refs[:7]
    if emit_cache:
        kc_ref, vc_ref = refs[7:9]

    x = x_ref[0]
    mod = mod_ref[0]
    sh1 = mod[:, 0:D_MODEL]
    sc1 = mod[:, D_MODEL:2 * D_MODEL]
    h = (x * (1.0 + sc1) + sh1).astype(BF16)

    def proj(lo, hi):
        return _dot(h, w_ref[:, lo:hi])

    def rms(t, g):
        return t * lax.rsqrt(jnp.mean(t * t, axis=-1, keepdims=True) + EPS) * g

    def rotate(t):
        return (t * cos_ref[...] + pltpu.roll(t, HEAD_DIM - ROPE_PAIRS, 1) * sina_ref[...]
                + pltpu.roll(t, ROPE_PAIRS, 1) * sinb_ref[...])

    qp = proj(0, SPLIT_Q)
    qg = qg_ref[...]
    for hd in range(N_Q_HEADS):
        sl = slice(hd * HEAD_DIM, (hd + 1) * HEAD_DIM)
        t = rms(qp[:, sl], qg)
        if rope:
            t = rotate(t)
        q_ref[0, hd] = (t * Q_PRESCALE).astype(BF16)

    kv = proj(SPLIT_Q, SPLIT_V)
    kg = kg_ref[...]
    for hd in range(N_KV_HEADS):
        sl = slice(hd * HEAD_DIM, (hd + 1) * HEAD_DIM)
        t = rms(kv[:, sl], kg)
        if emit_cache:
            kc_ref[0, :, sl] = t
        if rope:
            t = rotate(t)
        k_ref[0, :, sl] = t.astype(BF16)
    v = kv[:, KV_WIDTH:]
    if emit_cache:
        vc_ref[0] = v
    vt_ref[0] = v.T.astype(BF16)

    xp_ref[0] = proj(SPLIT_V, SPLIT_POOL)

    u_ref[0] = _gelu(proj(SPLIT_POOL, SPLIT_U)).astype(BF16)
    gv = _gelu(proj(SPLIT_U, SPLIT_SV))
    mu = jnp.mean(gv, axis=-1, keepdims=True)
    gc = gv - mu
    var = jnp.mean(gc * gc, axis=-1, keepdims=True)
    vn_ref[0] = (gc * lax.rsqrt(var + EPS)).astype(BF16)

    for br in range(N_BRANCHES):
        lo = SPLIT_SV + br * D_MODEL
        g_ref[0, :, br * D_MODEL:(br + 1) * D_MODEL] = jax.nn.sigmoid(proj(lo, lo + D_MODEL)).astype(BF16)


def _inproj(x, mod, w_in, qg, kg, rope_tabs, tm, emit_cache):
    b, s, _ = x.shape
    rope = rope_tabs is not None
    per_batch_mod = mod.shape[0] > 1
    tok = lambda w: pl.BlockSpec((1, tm, w), lambda bi, i: (bi, i, 0))
    in_specs = [tok(D_MODEL),
                pl.BlockSpec((1, 1, 6 * D_MODEL), (lambda bi, i: (bi, 0, 0)) if per_batch_mod else (lambda bi, i: (0, 0, 0))),
                _resident((D_MODEL, IN_WIDTH)), _resident((1, HEAD_DIM)), _resident((1, HEAD_DIM))]
    args = [x, mod, w_in, qg, kg]
    if rope:
        in_specs += [pl.BlockSpec((tm, HEAD_DIM), lambda bi, i: (i, 0))] * 3
        args += list(rope_tabs)
    out_specs = [pl.BlockSpec((1, N_Q_HEADS, tm, HEAD_DIM), lambda bi, i: (bi, 0, i, 0)), tok(KV_WIDTH),
                 pl.BlockSpec((1, KV_WIDTH, tm), lambda bi, i: (bi, 0, i)),
                 tok(POOL_WIDTH), tok(SGU_WIDTH), tok(SGU_WIDTH), tok(N_BRANCHES * D_MODEL)]
    sds = lambda w, dt: jax.ShapeDtypeStruct((b, s, w), dt)
    out_shape = [jax.ShapeDtypeStruct((b, N_Q_HEADS, s, HEAD_DIM), BF16), sds(KV_WIDTH, BF16),
                 jax.ShapeDtypeStruct((b, KV_WIDTH, s), BF16),
                 sds(POOL_WIDTH, F32), sds(SGU_WIDTH, BF16), sds(SGU_WIDTH, BF16), sds(N_BRANCHES * D_MODEL, BF16)]
    if emit_cache:
        out_specs += [tok(KV_WIDTH), tok(KV_WIDTH)]
        out_shape += [sds(KV_WIDTH, F32), sds(KV_WIDTH, F32)]
    return pl.pallas_call(
        functools.partial(_inproj_kernel, rope, emit_cache),
        grid=(b, s // tm), in_specs=in_specs, out_specs=out_specs, out_shape=out_shape,
        compiler_params=_params(2), name="inproj_rope" if rope else "inproj_ctx",
    )(*args)


def _attn_kernel(n_ctx, n_chunks, tkc, q_ref, k_ref, vt_ref, *refs):
    refs = list(refs)
    if n_ctx:
        ck_ref, cv_ref = refs[:2]
        refs = refs[2:]
    o_ref, kall, vtall, st_scr, m_scr, l_scr, acc_scr = refs
    tq = q_ref.shape[2]

    @pl.when(pl.program_id(1) == 0)
    def _stage_keys():
        if n_ctx:
            kall[0:n_ctx, :] = ck_ref[0].astype(BF16)
            cvt = cv_ref[0].T.astype(BF16)
        kall[n_ctx:, :] = k_ref[0]
        for c in range(n_chunks):
            lo, hi = c * tkc, (c + 1) * tkc
            if lo < n_ctx:
                end = min(hi, n_ctx)
                vtall[c, :, 0:end - lo] = cvt[:, lo:end]
            if hi > n_ctx:
                beg = max(lo, n_ctx)
                vtall[c, :, beg - lo:tkc] = vt_ref[0, :, beg - n_ctx:hi - n_ctx]

    m_scr[...] = jnp.full(m_scr.shape, -jnp.inf, F32)
    l_scr[...] = jnp.zeros(l_scr.shape, F32)
    acc_scr[...] = jnp.zeros(acc_scr.shape, F32)

    def scores(c, slot):
        for g in range(N_KV_HEADS):
            kc = kall[pl.ds(pl.multiple_of(c * tkc, tkc), tkc), g * HEAD_DIM:(g + 1) * HEAD_DIM]
            qg = q_ref[0, g * Q_PER_KV:(g + 1) * Q_PER_KV].reshape(Q_PER_KV * tq, HEAD_DIM)
            st_scr[slot, g] = lax.dot_general(kc, qg, (((1,), (1,)), ((), ())), preferred_element_type=F32)

    def accumulate(c, slot):
        for g in range(N_KV_HEADS):
            st = st_scr[slot, g]
            m_old = m_scr[g]
            m_new = jnp.maximum(m_old, jnp.max(st, axis=0, keepdims=True))
            p = jnp.exp2(st - m_new)
            alpha = jnp.exp2(m_old - m_new)
            l_scr[g] = alpha * l_scr[g] + jnp.sum(p, axis=0, keepdims=True)
            vtc = vtall[c, g * HEAD_DIM:(g + 1) * HEAD_DIM, :]
            acc_scr[g] = alpha * acc_scr[g] + _dot(vtc, p.astype(BF16))
            m_scr[g] = m_new

    scores(0, 0)
    n_pairs = (n_chunks - 1) // 2

    def pair_step(j, carry):
        c = 2 * j
        scores(c + 1, 1)
        accumulate(c, 0)
        scores(c + 2, 0)
        accumulate(c + 1, 1)
        return carry

    lax.fori_loop(0, n_pairs, pair_step, 0)
    c = 2 * n_pairs
    if n_chunks - 1 - c == 1:
        scores(c + 1, 1)
        accumulate(c, 0)
        accumulate(c + 1, 1)
    else:
        accumulate(c, 0)

    for g in range(N_KV_HEADS):
        ot = acc_scr[g] / l_scr[g]
        for j in range(Q_PER_KV):
            hd = g * Q_PER_KV + j
            o_ref[0, :, hd * HEAD_DIM:(hd + 1) * HEAD_DIM] = ot[:, j * tq:(j + 1) * tq].T.astype(BF16)


def _attention(q, k, vt, ctx_k, ctx_v, tq, tkc):
    b, _, s, _ = q.shape
    n_ctx = 0 if ctx_k is None else ctx_k.shape[1]
    total = n_ctx + s
    n_chunks = total // tkc
    in_specs = [pl.BlockSpec((1, N_Q_HEADS, tq, HEAD_DIM), lambda bi, i: (bi, 0, i, 0)),
                pl.BlockSpec((1, s, KV_WIDTH), lambda bi, i: (bi, 0, 0)),
                pl.BlockSpec((1, KV_WIDTH, s), lambda bi, i: (bi, 0, 0))]
    args = [q, k, vt]
    if n_ctx:
        in_specs += [pl.BlockSpec((1, n_ctx, KV_WIDTH), lambda bi, i: (bi, 0, 0))] * 2
        args += [ctx_k, ctx_v]
    lanes = Q_PER_KV * tq
    return pl.pallas_call(
        functools.partial(_attn_kernel, n_ctx, n_chunks, tkc),
        grid=(b, s // tq), in_specs=in_specs,
        out_specs=pl.BlockSpec((1, tq, ATTN_WIDTH), lambda bi, i: (bi, i, 0)),
        out_shape=jax.ShapeDtypeStruct((b, s, ATTN_WIDTH), BF16),
        scratch_shapes=[pltpu.VMEM((total, KV_WIDTH), BF16),
                        pltpu.VMEM((n_chunks, KV_WIDTH, tkc), BF16),
                        pltpu.VMEM((2, N_KV_HEADS, tkc, lanes), F32),
                        pltpu.VMEM((N_KV_HEADS, 1, lanes), F32),
                        pltpu.VMEM((N_KV_HEADS, 1, lanes), F32),
                        pltpu.VMEM((N_KV_HEADS, HEAD_DIM, lanes), F32)],
        compiler_params=_params(2), name="attn_lat" if n_ctx else "attn_ctx",
    )(*args)


def _mix_kernel(seq, x_ref, mod_ref, attn_ref, xp_ref, xprev_ref, xnext_ref, u_ref, vn_ref, g_ref,
                wao_ref, wpo_ref, wso_ref, wout_ref, wpg_ref, psc_ref, ws_ref, bs_ref, lng_ref, lnb_ref,
                o_ref, xe_scr, br_scr):
    tm = x_ref.shape[1]
    i = pl.program_id(1)
    nt = pl.num_programs(1)

    xe_scr[0:POOL_HALO, :] = jnp.where(i > 0, xprev_ref[0], 0.0)
    xe_scr[POOL_HALO:POOL_HALO + tm, :] = xp_ref[0]
    xe_scr[POOL_HALO + tm:, :] = jnp.where(i < nt - 1, xnext_ref[0], 0.0)
    pos = i * tm + lax.broadcasted_iota(jnp.int32, (tm, 1), 0)
    for g, w in enumerate(POOL_WINDOWS):
        gl = slice(g * POOL_GROUP_DIM, (g + 1) * POOL_GROUP_DIM)
        sums = xe_scr[pl.ds(POOL_HALO - w // 2, tm), gl]
        for d in range(1 - w // 2, w - w // 2):
            sums = sums + xe_scr[pl.ds(POOL_HALO + d, tm), gl]
        cnt = (jnp.minimum(pos + (w - w // 2), seq) - jnp.maximum(pos - w // 2, 0)).astype(F32)
        pooled = (sums / cnt - xp_ref[0, :, gl]).astype(BF16)
        br_scr[:, gl] = (_dot(pooled, wpg_ref[g]) * psc_ref[:, gl]).astype(BF16)
    gates = g_ref[0]
    merged = gates[:, D_MODEL:2 * D_MODEL].astype(F32) * _dot(br_scr[...], wpo_ref[...])

    for c in range(tm // CHUNK):
        rows = slice(c * CHUNK, (c + 1) * CHUNK)
        for g in range(N_SGU_GROUPS):
            gl = slice(g * SGU_GROUP_DIM, (g + 1) * SGU_GROUP_DIM)
            mixed = _dot(ws_ref[g], vn_ref[0, rows, gl]) + bs_ref[:, g:g + 1]
            br_scr[rows, gl] = (u_ref[0, rows, gl].astype(F32) * mixed).astype(BF16)
    merged = merged + gates[:, 2 * D_MODEL:].astype(F32) * _dot(br_scr[...], wso_ref[...])

    merged = merged + gates[:, 0:D_MODEL].astype(F32) * _dot(attn_ref[0], wao_ref[...])
    mix = _dot(merged.astype(BF16), wout_ref[...])
    g1 = mod_ref[0][:, 2 * D_MODEL:3 * D_MODEL]
    o_ref[0] = _layer_norm(DEEPNORM_ALPHA * x_ref[0] + g1 * mix, lng_ref[...], lnb_ref[...])


def _mix(x, mod, attn, xp, u, vn, gates, w, tm):
    b, s, _ = x.shape
    per_batch_mod = mod.shape[0] > 1
    hb = tm // POOL_HALO
    last_halo = s // POOL_HALO - 1
    tok = lambda wd: pl.BlockSpec((1, tm, wd), lambda bi, i: (bi, i, 0))
    in_specs = [tok(D_MODEL),
                pl.BlockSpec((1, 1, 6 * D_MODEL), (lambda bi, i: (bi, 0, 0)) if per_batch_mod else (lambda bi, i: (0, 0, 0))),
                tok(ATTN_WIDTH), tok(POOL_WIDTH),
                pl.BlockSpec((1, POOL_HALO, POOL_WIDTH), lambda bi, i: (bi, jnp.maximum(i * hb - 1, 0), 0)),
                pl.BlockSpec((1, POOL_HALO, POOL_WIDTH), lambda bi, i: (bi, jnp.minimum((i + 1) * hb, last_halo), 0)),
                tok(SGU_WIDTH), tok(SGU_WIDTH), tok(N_BRANCHES * D_MODEL),
                _resident((ATTN_WIDTH, D_MODEL)), _resident((POOL_WIDTH, D_MODEL)), _resident((SGU_WIDTH, D_MODEL)),
                _resident((D_MODEL, D_MODEL)), _resident((N_POOL_GROUPS, POOL_GROUP_DIM, POOL_GROUP_DIM)),
                _resident((1, POOL_WIDTH)), _resident((N_SGU_GROUPS, CHUNK, CHUNK)), _resident((CHUNK, N_SGU_GROUPS)),
                _resident((1, D_MODEL)), _resident((1, D_MODEL))]
    return pl.pallas_call(
        functools.partial(_mix_kernel, s),
        grid=(b, s // tm), in_specs=in_specs, out_specs=tok(D_MODEL),
        out_shape=jax.ShapeDtypeStruct((b, s, D_MODEL), F32),
        scratch_shapes=[pltpu.VMEM((tm + 2 * POOL_HALO, POOL_WIDTH), F32), pltpu.VMEM((tm, POOL_WIDTH), BF16)],
        compiler_params=_params(2), name="mix",
    )(x, mod, attn, xp, xp, xp, u, vn, gates, w["attn_o"], w["pool_o"], w["sgu_o"], w["out"], w["pool_g"],
      w["pool_scale"], w["sgu"], w["b_sgu"], w["ln1_g"], w["ln1_b"])


def _ffn_kernel(x_ref, mod_ref, wa_ref, wb_ref, wo_ref, lng_ref, lnb_ref, o_ref):
    x = x_ref[0]
    mod = mod_ref[0]
    sh2 = mod[:, 3 * D_MODEL:4 * D_MODEL]
    sc2 = mod[:, 4 * D_MODEL:5 * D_MODEL]
    g2 = mod[:, 5 * D_MODEL:6 * D_MODEL]
    h = (x * (1.0 + sc2) + sh2).astype(BF16)
    step = D_FF // FFN_CHUNKS
    f = None
    for j in range(FFN_CHUNKS):
        sl = slice(j * step, (j + 1) * step)
        a = _dot(h, wa_ref[:, sl])
        bgate = _dot(h, wb_ref[:, sl])
        part = _dot((a * jax.nn.sigmoid(a) * bgate).astype(BF16), wo_ref[sl, :])
        f = part if f is None else f + part
    o_ref[0] = _layer_norm(DEEPNORM_ALPHA * x + g2 * f, lng_ref[...], lnb_ref[...])


def _ffn(x, mod, w, tm):
    b, s, _ = x.shape
    per_batch_mod = mod.shape[0] > 1
    tok = pl.BlockSpec((1, tm, D_MODEL), lambda bi, i: (bi, i, 0))
    in_specs = [tok,
                pl.BlockSpec((1, 1, 6 * D_MODEL), (lambda bi, i: (bi, 0, 0)) if per_batch_mod else (lambda bi, i: (0, 0, 0))),
                _resident((D_MODEL, D_FF)), _resident((D_MODEL, D_FF)), _resident((D_FF, D_MODEL)),
                _resident((1, D_MODEL)), _resident((1, D_MODEL))]
    return pl.pallas_call(
        _ffn_kernel, grid=(b, s // tm), in_specs=in_specs, out_specs=tok,
        out_shape=jax.ShapeDtypeStruct((b, s, D_MODEL), F32),
        compiler_params=_params(2), name="ffn",
    )(x, mod, w["ffn_a"], w["ffn_b"], w["ffn_o"], w["ln2_g"], w["ln2_b"])


def _rope_tables(seq):
    rows = seq // GRID_W
    row = jnp.repeat(jnp.arange(rows), GRID_W).astype(F32)
    col = jnp.tile(jnp.arange(GRID_W), rows).astype(F32)
    inv_freq = ROPE_THETA ** (-jnp.arange(ROPE_PAIRS, dtype=F32) / ROPE_PAIRS)
    ar = row[:, None] * inv_freq
    ac = col[:, None] * inv_freq
    z = jnp.zeros_like(ar)
    cos = jnp.concatenate([jnp.cos(ar), jnp.cos(ar), jnp.cos(ac), jnp.cos(ac)], axis=1)
    sina = jnp.concatenate([-jnp.sin(ar), z, -jnp.sin(ac), z], axis=1)
    sinb = jnp.concatenate([z, jnp.sin(ar), z, jnp.sin(ac)], axis=1)
    return cos, sina, sinb


def _trunk_layer(x, mod, rope_tabs, ctx_k, ctx_v, w, tm, emit_cache):
    outs = _inproj(x, mod, w["in"], w["q_norm_g"], w["k_norm_g"], rope_tabs, tm, emit_cache)
    q, k, vt, xp, u, vn, gates = outs[:7]
    s = x.shape[1]
    attn = _attention(q, k, vt, ctx_k, ctx_v, min(ATTN_Q_TILE, s), min(ATTN_KV_CHUNK, s))
    x = _mix(x, mod, attn, xp, u, vn, gates, w, tm)
    x = _ffn(x, mod, w, tm)
    return x, outs[7:]


def kernel(x_prompt, x_sample, cache_k, cache_v, c, c_ctx, w_ada, b_ada, w_in, q_norm_g, k_norm_g, w_pool_g,
           pool_scale, w_sgu, b_sgu, w_attn_o, w_pool_o, w_sgu_o, w_out, ln1_g, ln1_b, w_ffn_in, w_ffn_out,
           ln2_g, ln2_b):
    n_dec = c.shape[0]
    cvec = jnp.concatenate([c, c_ctx[None, :], jnp.zeros((ADA_ROWS - n_dec - 1, D_MODEL), F32)], axis=0)
    mods = _ada_rows(cvec, w_ada, b_ada)
    rope_tabs = _rope_tables(x_sample.shape[1])
    n_past = cache_k.shape[2]
    y_p, y_s = x_prompt, x_sample
    new_k, new_v = [], []
    for l in range(DEPTH):
        row = lambda a: a[l].reshape(1, -1)
        w = {
            "in": w_in[l].astype(BF16), "q_norm_g": row(q_norm_g), "k_norm_g": row(k_norm_g),
            "pool_g": w_pool_g[l].astype(BF16), "pool_scale": row(pool_scale),
            "sgu": w_sgu[l].astype(BF16), "b_sgu": b_sgu[l].T,
            "attn_o": w_attn_o[l].astype(BF16), "pool_o": w_pool_o[l].astype(BF16),
            "sgu_o": w_sgu_o[l].astype(BF16), "out": w_out[l].astype(BF16),
            "ln1_g": row(ln1_g), "ln1_b": row(ln1_b),
            "ffn_a": w_ffn_in[l, :, :D_FF].astype(BF16), "ffn_b": w_ffn_in[l, :, D_FF:].astype(BF16),
            "ffn_o": w_ffn_out[l].astype(BF16), "ln2_g": row(ln2_g), "ln2_b": row(ln2_b),
        }
        mod_lat = mods[l, :n_dec].reshape(n_dec, 1, 6 * D_MODEL)
        mod_ctx = mods[l, n_dec:n_dec + 1].reshape(1, 1, 6 * D_MODEL)
        y_p, (k_ctx, v_ctx) = _trunk_layer(y_p, mod_ctx, None, None, None, w, x_prompt.shape[1], True)
        new_k.append(k_ctx)
        new_v.append(v_ctx)
        ctx_k = cache_k[:, l].reshape(n_dec, n_past, KV_WIDTH)
        ctx_v = cache_v[:, l].reshape(n_dec, n_past, KV_WIDTH)
        y_s, _ = _trunk_layer(y_s, mod_lat, rope_tabs, ctx_k, ctx_v, w, TOKEN_TILE, False)
    cache_shape = (x_prompt.shape[0], DEPTH, x_prompt.shape[1], N_KV_HEADS, HEAD_DIM)
    new_cache_k = jnp.stack(new_k, axis=1).reshape(cache_shape)
    new_cache_v = jnp.stack(new_v, axis=1).reshape(cache_shape)
    return (y_p, y_s, new_cache_k, new_cache_v)
```

```python
import functools
import math

import jax
import jax.numpy as jnp
from jax import lax
from jax.experimental import pallas as pl
from jax.experimental.pallas import tpu as pltpu

D_MODEL = 1024
DEPTH = 2
GRID_W = 64
HEAD_DIM = 128
N_Q_HEADS = 8
N_KV_HEADS = 2
Q_PER_KV = N_Q_HEADS // N_KV_HEADS
ATTN_WIDTH = N_Q_HEADS * HEAD_DIM
KV_WIDTH = N_KV_HEADS * HEAD_DIM
ROPE_THETA = 10000.0
ROPE_PAIRS = HEAD_DIM // 4
POOL_WINDOWS = (2, 4, 8, 16)
N_POOL_GROUPS = 4
POOL_WIDTH = D_MODEL // 2
POOL_GROUP_DIM = POOL_WIDTH // N_POOL_GROUPS
POOL_HALO = max(POOL_WINDOWS) // 2
CHUNK = 128
N_SGU_GROUPS = 4
SGU_WIDTH = D_MODEL // 2
SGU_GROUP_DIM = SGU_WIDTH // N_SGU_GROUPS
N_BRANCHES = 3
IN_WIDTH = ATTN_WIDTH + 2 * KV_WIDTH + POOL_WIDTH + 2 * SGU_WIDTH + N_BRANCHES * D_MODEL
SPLIT_Q = ATTN_WIDTH
SPLIT_K = SPLIT_Q + KV_WIDTH
SPLIT_V = SPLIT_K + KV_WIDTH
SPLIT_POOL = SPLIT_V + POOL_WIDTH
SPLIT_U = SPLIT_POOL + SGU_WIDTH
SPLIT_SV = SPLIT_U + SGU_WIDTH
D_FF = ((8 * D_MODEL + 3 * 256 - 1) // (3 * 256)) * 256
DEEPNORM_ALPHA = (2 * DEPTH) ** 0.25
EPS = 1e-6

Q_PRESCALE = HEAD_DIM ** -0.5 * math.log2(math.e)
SQRT_HALF = 0.5 ** 0.5

V7X_VMEM_LIMIT_BYTES = 56 * 1024 * 1024
ADA_ROWS = 16
ADA_TN = 1536
TOKEN_TILE = 512
ATTN_Q_TILE = 512
ATTN_KV_CHUNK = 768
ATTN_HEADS_PER_UNIT = 1
TAIL_ROW_BLOCKS = 2
FFN_CHUNK_EDGES = (0, 1280, D_FF)

BF16 = jnp.bfloat16
F32 = jnp.float32


def _dot(a, b):
    return jnp.dot(a, b, preferred_element_type=F32)


def _resident(shape):
    nd = len(shape)
    return pl.BlockSpec(shape, lambda *_: (0,) * nd, pipeline_mode=pl.Buffered(1))


def _params(n_grid):
    return pltpu.CompilerParams(dimension_semantics=("arbitrary",) * n_grid,
                                vmem_limit_bytes=V7X_VMEM_LIMIT_BYTES)


def _layer_norm(y, g, b):
    mu = jnp.mean(y, axis=-1, keepdims=True)
    yc = y - mu
    var = jnp.mean(yc * yc, axis=-1, keepdims=True)
    return yc * lax.rsqrt(var + EPS) * g + b


def _gelu(x):
    return 0.5 * x * (1.0 + lax.erf(x * SQRT_HALF))


def _ada_kernel(c_ref, w_ref, b_ref, o_ref):
    c = c_ref[...]
    s = c * jax.nn.sigmoid(c)
    o_ref[0] = _dot(s.astype(BF16), w_ref[0].astype(BF16)) + b_ref[0]


def _ada_rows(cvec, w_ada, b_ada):
    n = 6 * D_MODEL
    return pl.pallas_call(
        _ada_kernel,
        grid=(DEPTH, n // ADA_TN),
        in_specs=[pl.BlockSpec((ADA_ROWS, D_MODEL), lambda l, j: (0, 0)),
                  pl.BlockSpec((1, D_MODEL, ADA_TN), lambda l, j: (l, 0, j)),
                  pl.BlockSpec((1, 1, ADA_TN), lambda l, j: (l, 0, j))],
        out_specs=pl.BlockSpec((1, ADA_ROWS, ADA_TN), lambda l, j: (l, 0, j)),
        out_shape=jax.ShapeDtypeStruct((DEPTH, ADA_ROWS, n), F32),
        compiler_params=_params(2),
        name="ada_rows",
    )(cvec, w_ada, b_ada.reshape(DEPTH, 1, n))


def _inproj_kernel(rope, emit_cache, x_ref, mod_ref, w_ref, qg_ref, kg_ref, *refs):
    refs = list(refs)
    if rope:
        cos_ref, sina_ref, sinb_ref = refs[:3]
        refs = refs[3:]
    q_ref, k_ref, vt_ref, xp_ref, u_ref, vn_ref, g_ref = refs[:7]
    if emit_cache:
        kc_ref, vc_ref = refs[7:9]

    x = x_ref[0]
    mod = mod_ref[0]
    sh1 = mod[:, 0:D_MODEL]
    sc1 = mod[:, D_MODEL:2 * D_MODEL]
    h = (x * (1.0 + sc1) + sh1).astype(BF16)

    def proj(lo, hi):
        return _dot(h, w_ref[:, lo:hi])

    def rms(t, g):
        return t * lax.rsqrt(jnp.mean(t * t, axis=-1, keepdims=True) + EPS) * g

    def rotate(t):
        return (t * cos_ref[...] + pltpu.roll(t, HEAD_DIM - ROPE_PAIRS, 1) * sina_ref[...]
                + pltpu.roll(t, ROPE_PAIRS, 1) * sinb_ref[...])

    u_ref[0] = _gelu(proj(SPLIT_POOL, SPLIT_U)).astype(BF16)
    gv = _gelu(proj(SPLIT_U, SPLIT_SV))
    mu = jnp.mean(gv, axis=-1, keepdims=True)
    gc = gv - mu
    var = jnp.mean(gc * gc, axis=-1, keepdims=True)
    vn_ref[0] = (gc * lax.rsqrt(var + EPS)).astype(BF16)

    qp = proj(0, SPLIT_Q)
    qg = qg_ref[...]
    for hd in range(N_Q_HEADS):
        sl = slice(hd * HEAD_DIM, (hd + 1) * HEAD_DIM)
        t = rms(qp[:, sl], qg)
        if rope:
            t = rotate(t)
        q_ref[0, hd] = (t * Q_PRESCALE).astype(BF16)

    kv = proj(SPLIT_Q, SPLIT_V)
    kg = kg_ref[...]
    for hd in range(N_KV_HEADS):
        sl = slice(hd * HEAD_DIM, (hd + 1) * HEAD_DIM)
        t = rms(kv[:, sl], kg)
        if emit_cache:
            kc_ref[0, :, sl] = t
        if rope:
            t = rotate(t)
        k_ref[0, :, sl] = t.astype(BF16)
    v = kv[:, KV_WIDTH:]
    if emit_cache:
        vc_ref[0] = v
    vt_ref[0] = v.T.astype(BF16)

    for br in range(N_BRANCHES):
        lo = SPLIT_SV + br * D_MODEL
        g_ref[0, :, br * D_MODEL:(br + 1) * D_MODEL] = jax.nn.sigmoid(proj(lo, lo + D_MODEL)).astype(BF16)

    xp_ref[0] = proj(SPLIT_V, SPLIT_POOL)


def _inproj(x, mod, w_in, qg, kg, rope_tabs, tm, emit_cache):
    b, s, _ = x.shape
    rope = rope_tabs is not None
    per_batch_mod = mod.shape[0] > 1
    tok = lambda w: pl.BlockSpec((1, tm, w), lambda bi, i: (bi, i, 0))
    in_specs = [tok(D_MODEL),
                pl.BlockSpec((1, 1, 6 * D_MODEL), (lambda bi, i: (bi, 0, 0)) if per_batch_mod else (lambda bi, i: (0, 0, 0))),
                _resident((D_MODEL, IN_WIDTH)), _resident((1, HEAD_DIM)), _resident((1, HEAD_DIM))]
    args = [x, mod, w_in, qg, kg]
    if rope:
        in_specs += [pl.BlockSpec((tm, HEAD_DIM), lambda bi, i: (i, 0))] * 3
        args += list(rope_tabs)
    out_specs = [pl.BlockSpec((1, N_Q_HEADS, tm, HEAD_DIM), lambda bi, i: (bi, 0, i, 0)), tok(KV_WIDTH),
                 pl.BlockSpec((1, KV_WIDTH, tm), lambda bi, i: (bi, 0, i)),
                 tok(POOL_WIDTH), tok(SGU_WIDTH), tok(SGU_WIDTH), tok(N_BRANCHES * D_MODEL)]
    sds = lambda w, dt: jax.ShapeDtypeStruct((b, s, w), dt)
    out_shape = [jax.ShapeDtypeStruct((b, N_Q_HEADS, s, HEAD_DIM), BF16), sds(KV_WIDTH, BF16),
                 jax.ShapeDtypeStruct((b, KV_WIDTH, s), BF16),
                 sds(POOL_WIDTH, F32), sds(SGU_WIDTH, BF16), sds(SGU_WIDTH, BF16), sds(N_BRANCHES * D_MODEL, BF16)]
    if emit_cache:
        out_specs += [tok(KV_WIDTH), tok(KV_WIDTH)]
        out_shape += [sds(KV_WIDTH, F32), sds(KV_WIDTH, F32)]
    return pl.pallas_call(
        functools.partial(_inproj_kernel, rope, emit_cache),
        grid=(b, s // tm), in_specs=in_specs, out_specs=out_specs, out_shape=out_shape,
        compiler_params=_params(2), name="inproj_rope" if rope else "inproj_ctx",
    )(*args)


def _attn_kernel(n_ctx, n_chunks, tkc, q_ref, k_ref, vt_ref, *refs):
    refs = list(refs)
    if n_ctx:
        ck_ref, cv_ref = refs[:2]
        refs = refs[2:]
    o_ref, kall, vtall, st_scr, cm_scr, m_scr, l_scr, acc_scr = refs
    tq = q_ref.shape[2]

    @pl.when(pl.program_id(1) == 0)
    def _stage_keys():
        if n_ctx:
            kall[0:n_ctx, :] = ck_ref[0].astype(BF16)
            cvt = cv_ref[0].T.astype(BF16)
        kall[n_ctx:, :] = k_ref[0]
        for c in range(n_chunks):
            lo, hi = c * tkc, (c + 1) * tkc
            if lo < n_ctx:
                end = min(hi, n_ctx)
                vtall[c, :, 0:end - lo] = cvt[:, lo:end]
            if hi > n_ctx:
                beg = max(lo, n_ctx)
                vtall[c, :, beg - lo:tkc] = vt_ref[0, :, beg - n_ctx:hi - n_ctx]

    m_scr[...] = jnp.full(m_scr.shape, -jnp.inf, F32)
    l_scr[...] = jnp.zeros(l_scr.shape, F32)
    acc_scr[...] = jnp.zeros(acc_scr.shape, F32)

    n_units = m_scr.shape[0]
    hpu = N_Q_HEADS // n_units

    def scores(c, slot, u):
        g = u * hpu // Q_PER_KV
        kc = kall[pl.ds(pl.multiple_of(c * tkc, tkc), tkc), g * HEAD_DIM:(g + 1) * HEAD_DIM]
        qu = q_ref[0, u * hpu:(u + 1) * hpu].reshape(hpu * tq, HEAD_DIM)
        st = lax.dot_general(kc, qu, (((1,), (1,)), ((), ())), preferred_element_type=F32)
        st_scr[slot, u] = st
        cm_scr[slot, u] = jnp.max(st, axis=0, keepdims=True)

    def accumulate(c, slot, u):
        g = u * hpu // Q_PER_KV
        m_old = m_scr[u]
        m_new = jnp.maximum(m_old, cm_scr[slot, u])
        p = jnp.exp2(st_scr[slot, u] - m_new)
        alpha = jnp.exp2(m_old - m_new)
        l_scr[u] = alpha * l_scr[u] + jnp.sum(p, axis=0, keepdims=True)
        vtc = vtall[c, g * HEAD_DIM:(g + 1) * HEAD_DIM, :]
        acc_scr[u] = alpha * acc_scr[u] + _dot(vtc, p.astype(BF16))
        m_scr[u] = m_new

    def step(c_acc, slot_acc, c_next):
        for u in range(n_units):
            if c_next is not None:
                scores(c_next, 1 - slot_acc, u)
            accumulate(c_acc, slot_acc, u)

    for u in range(n_units):
        scores(0, 0, u)
    n_pairs = (n_chunks - 1) // 2

    def pair_step(j, carry):
        c = 2 * j
        step(c, 0, c + 1)
        step(c + 1, 1, c + 2)
        return carry

    lax.fori_loop(0, n_pairs, pair_step, 0)
    c = 2 * n_pairs
    if n_chunks - 1 - c == 1:
        step(c, 0, c + 1)
        step(c + 1, 1, None)
    else:
        step(c, 0, None)

    for u in range(n_units):
        ot = acc_scr[u] / l_scr[u]
        for j in range(hpu):
            hd = u * hpu + j
            o_ref[0, :, hd * HEAD_DIM:(hd + 1) * HEAD_DIM] = ot[:, j * tq:(j + 1) * tq].T.astype(BF16)


def _attention(q, k, vt, ctx_k, ctx_v, tq, tkc):
    b, _, s, _ = q.shape
    n_ctx = 0 if ctx_k is None else ctx_k.shape[1]
    total = n_ctx + s
    n_chunks = total // tkc
    in_specs = [pl.BlockSpec((1, N_Q_HEADS, tq, HEAD_DIM), lambda bi, i: (bi, 0, i, 0)),
                pl.BlockSpec((1, s, KV_WIDTH), lambda bi, i: (bi, 0, 0)),
                pl.BlockSpec((1, KV_WIDTH, s), lambda bi, i: (bi, 0, 0))]
    args = [q, k, vt]
    if n_ctx:
        in_specs += [pl.BlockSpec((1, n_ctx, KV_WIDTH), lambda bi, i: (bi, 0, 0))] * 2
        args += [ctx_k, ctx_v]
    n_units = N_Q_HEADS // ATTN_HEADS_PER_UNIT
    lanes = ATTN_HEADS_PER_UNIT * tq
    return pl.pallas_call(
        functools.partial(_attn_kernel, n_ctx, n_chunks, tkc),
        grid=(b, s // tq), in_specs=in_specs,
        out_specs=pl.BlockSpec((1, tq, ATTN_WIDTH), lambda bi, i: (bi, i, 0)),
        out_shape=jax.ShapeDtypeStruct((b, s, ATTN_WIDTH), BF16),
        scratch_shapes=[pltpu.VMEM((total, KV_WIDTH), BF16),
                        pltpu.VMEM((n_chunks, KV_WIDTH, tkc), BF16),
                        pltpu.VMEM((2, n_units, tkc, lanes), F32),
                        pltpu.VMEM((2, n_units, 1, lanes), F32),
                        pltpu.VMEM((n_units, 1, lanes), F32),
                        pltpu.VMEM((n_units, 1, lanes), F32),
                        pltpu.VMEM((n_units, HEAD_DIM, lanes), F32)],
        compiler_params=_params(2), name="attn_lat" if n_ctx else "attn_ctx",
    )(*args)


def _mix_kernel(x_ref, mod_ref, attn_ref, xp_ref, xprev_ref, xnext_ref, cnt_ref, u_ref, vn_ref, g_ref,
                wao_ref, wpo_ref, wso_ref, wout_ref, wpg_ref, psc_ref, ws_ref, bs_ref, lng_ref, lnb_ref,
                o_ref, pool_scr, sgu_scr):
    tm = x_ref.shape[1]
    i = pl.program_id(1)
    nt = pl.num_programs(1)
    xc = xp_ref[0]
    xe = jnp.concatenate([jnp.where(i > 0, xprev_ref[0], 0.0), xc, jnp.where(i < nt - 1, xnext_ref[0], 0.0)], axis=0)
    n_rows = tm + 2 * POOL_HALO
    up = lambda a, s: pltpu.roll(a, n_rows - s, 0)
    n_c = tm // CHUNK
    attn = attn_ref[0]
    cols = D_MODEL // N_POOL_GROUPS
    attn_parts = []
    for g, w in enumerate(POOL_WINDOWS):
        gl = slice(g * POOL_GROUP_DIM, (g + 1) * POOL_GROUP_DIM)
        acc = xe[:, gl]
        span = 1
        while span < w:
            acc = acc + up(acc, span)
            span *= 2
        first = POOL_HALO - w // 2
        sums = (up(acc, first) if first else acc)[0:tm]
        pooled = (sums / cnt_ref[:, gl] - xc[:, gl]).astype(BF16)
        pool_scr[:, gl] = (_dot(pooled, wpg_ref[g]) * psc_ref[:, gl]).astype(BF16)

        cs = slice(g * cols, (g + 1) * cols)
        attn_parts.append(g_ref[0, :, cs].astype(F32) * _dot(attn, wao_ref[:, cs]))

        vcat = jnp.concatenate([vn_ref[0, c * CHUNK:(c + 1) * CHUNK, gl] for c in range(n_c)], axis=1)
        mixed = _dot(ws_ref[g], vcat) + bs_ref[:, g:g + 1]
        for c in range(n_c):
            rows = slice(c * CHUNK, (c + 1) * CHUNK)
            sgu_scr[rows, gl] = (u_ref[0, rows, gl].astype(F32) * mixed[:, c * CHUNK:(c + 1) * CHUNK]).astype(BF16)

    merged = jnp.concatenate(attn_parts, axis=1)
    merged = merged + g_ref[0, :, D_MODEL:2 * D_MODEL].astype(F32) * _dot(pool_scr[...], wpo_ref[...])
    merged = merged + g_ref[0, :, 2 * D_MODEL:].astype(F32) * _dot(sgu_scr[...], wso_ref[...])
    merged = merged.astype(BF16)
    g1 = mod_ref[0][:, 2 * D_MODEL:3 * D_MODEL]
    rb = tm // TAIL_ROW_BLOCKS
    for r in range(TAIL_ROW_BLOCKS):
        rows = slice(r * rb, (r + 1) * rb)
        mix = _dot(merged[rows], wout_ref[...])
        o_ref[0, rows, :] = _layer_norm(DEEPNORM_ALPHA * x_ref[0, rows, :] + g1 * mix, lng_ref[...], lnb_ref[...])


def _pool_counts(seq):
    t = jnp.arange(seq, dtype=jnp.int32)[:, None]
    w = jnp.repeat(jnp.array(POOL_WINDOWS, dtype=jnp.int32), POOL_GROUP_DIM)[None, :]
    return (jnp.minimum(t + (w - w // 2), seq) - jnp.maximum(t - w // 2, 0)).astype(F32)


def _mix(x, mod, attn, xp, u, vn, gates, w, tm):
    b, s, _ = x.shape
    per_batch_mod = mod.shape[0] > 1
    hb = tm // POOL_HALO
    last_halo = s // POOL_HALO - 1
    tok = lambda wd: pl.BlockSpec((1, tm, wd), lambda bi, i: (bi, i, 0))
    in_specs = [tok(D_MODEL),
                pl.BlockSpec((1, 1, 6 * D_MODEL), (lambda bi, i: (bi, 0, 0)) if per_batch_mod else (lambda bi, i: (0, 0, 0))),
                tok(ATTN_WIDTH), tok(POOL_WIDTH),
                pl.BlockSpec((1, POOL_HALO, POOL_WIDTH), lambda bi, i: (bi, jnp.maximum(i * hb - 1, 0), 0)),
                pl.BlockSpec((1, POOL_HALO, POOL_WIDTH), lambda bi, i: (bi, jnp.minimum((i + 1) * hb, last_halo), 0)),
                pl.BlockSpec((tm, POOL_WIDTH), lambda bi, i: (i, 0)),
                tok(SGU_WIDTH), tok(SGU_WIDTH), tok(N_BRANCHES * D_MODEL),
                _resident((ATTN_WIDTH, D_MODEL)), _resident((POOL_WIDTH, D_MODEL)), _resident((SGU_WIDTH, D_MODEL)),
                _resident((D_MODEL, D_MODEL)), _resident((N_POOL_GROUPS, POOL_GROUP_DIM, POOL_GROUP_DIM)),
                _resident((1, POOL_WIDTH)), _resident((N_SGU_GROUPS, CHUNK, CHUNK)), _resident((CHUNK, N_SGU_GROUPS)),
                _resident((1, D_MODEL)), _resident((1, D_MODEL))]
    return pl.pallas_call(
        _mix_kernel,
        grid=(b, s // tm), in_specs=in_specs, out_specs=tok(D_MODEL),
        out_shape=jax.ShapeDtypeStruct((b, s, D_MODEL), F32),
        scratch_shapes=[pltpu.VMEM((tm, POOL_WIDTH), BF16), pltpu.VMEM((tm, SGU_WIDTH), BF16)],
        compiler_params=_params(2), name="mix",
    )(x, mod, attn, xp, xp, xp, _pool_counts(s), u, vn, gates, w["attn_o"], w["pool_o"], w["sgu_o"], w["out"], w["pool_g"],
      w["pool_scale"], w["sgu"], w["b_sgu"], w["ln1_g"], w["ln1_b"])


def _ffn_kernel(x_ref, mod_ref, wa_ref, wb_ref, wo_ref, lng_ref, lnb_ref, o_ref):
    x = x_ref[0]
    tm = x.shape[0]
    mod = mod_ref[0]
    sh2 = mod[:, 3 * D_MODEL:4 * D_MODEL]
    sc2 = mod[:, 4 * D_MODEL:5 * D_MODEL]
    g2 = mod[:, 5 * D_MODEL:6 * D_MODEL]
    h = (x * (1.0 + sc2) + sh2).astype(BF16)

    def hidden(lo, hi):
        a = _dot(h, wa_ref[:, lo:hi])
        bgate = _dot(h, wb_ref[:, lo:hi])
        return (a * jax.nn.sigmoid(a) * bgate).astype(BF16)

    edges = list(zip(FFN_CHUNK_EDGES[:-1], FFN_CHUNK_EDGES[1:]))
    f = None
    for lo, hi in edges[:-1]:
        part = _dot(hidden(lo, hi), wo_ref[lo:hi, :])
        f = part if f is None else f + part
    lo, hi = edges[-1]
    act = hidden(lo, hi)
    rb = tm // TAIL_ROW_BLOCKS
    for r in range(TAIL_ROW_BLOCKS):
        rows = slice(r * rb, (r + 1) * rb)
        part = _dot(act[rows], wo_ref[lo:hi, :])
        fr = part if f is None else f[rows] + part
        o_ref[0, rows, :] = _layer_norm(DEEPNORM_ALPHA * x[rows] + g2 * fr, lng_ref[...], lnb_ref[...])


def _ffn(x, mod, w, tm):
    b, s, _ = x.shape
    per_batch_mod = mod.shape[0] > 1
    tok = pl.BlockSpec((1, tm, D_MODEL), lambda bi, i: (bi, i, 0))
    in_specs = [tok,
                pl.BlockSpec((1, 1, 6 * D_MODEL), (lambda bi, i: (bi, 0, 0)) if per_batch_mod else (lambda bi, i: (0, 0, 0))),
                _resident((D_MODEL, D_FF)), _resident((D_MODEL, D_FF)), _resident((D_FF, D_MODEL)),
                _resident((1, D_MODEL)), _resident((1, D_MODEL))]
    return pl.pallas_call(
        _ffn_kernel, grid=(b, s // tm), in_specs=in_specs, out_specs=tok,
        out_shape=jax.ShapeDtypeStruct((b, s, D_MODEL), F32),
        compiler_params=_params(2), name="ffn",
    )(x, mod, w["ffn_a"], w["ffn_b"], w["ffn_o"], w["ln2_g"], w["ln2_b"])


def _rope_tables(seq):
    rows = seq // GRID_W
    row = jnp.repeat(jnp.arange(rows), GRID_W).astype(F32)
    col = jnp.tile(jnp.arange(GRID_W), rows).astype(F32)
    inv_freq = ROPE_THETA ** (-jnp.arange(ROPE_PAIRS, dtype=F32) / ROPE_PAIRS)
    ar = row[:, None] * inv_freq
    ac = col[:, None] * inv_freq
    z = jnp.zeros_like(ar)
    cos = jnp.concatenate([jnp.cos(ar), jnp.cos(ar), jnp.cos(ac), jnp.cos(ac)], axis=1)
    sina = jnp.concatenate([-jnp.sin(ar), z, -jnp.sin(ac), z], axis=1)
    sinb = jnp.concatenate([z, jnp.sin(ar), z, jnp.sin(ac)], axis=1)
    return cos, sina, sinb


def _trunk_layer(x, mod, rope_tabs, ctx_k, ctx_v, w, tm, emit_cache):
    outs = _inproj(x, mod, w["in"], w["q_norm_g"], w["k_norm_g"], rope_tabs, tm, emit_cache)
    q, k, vt, xp, u, vn, gates = outs[:7]
    s = x.shape[1]
    attn = _attention(q, k, vt, ctx_k, ctx_v, min(ATTN_Q_TILE, s), min(ATTN_KV_CHUNK, s))
    x = _mix(x, mod, attn, xp, u, vn, gates, w, tm)
    x = _ffn(x, mod, w, tm)
    return x, outs[7:]


def kernel(x_prompt, x_sample, cache_k, cache_v, c, c_ctx, w_ada, b_ada, w_in, q_norm_g, k_norm_g, w_pool_g,
           pool_scale, w_sgu, b_sgu, w_attn_o, w_pool_o, w_sgu_o, w_out, ln1_g, ln1_b, w_ffn_in, w_ffn_out,
           ln2_g, ln2_b):
    n_dec = c.shape[0]
    cvec = jnp.concatenate([c, c_ctx[None, :], jnp.zeros((ADA_ROWS - n_dec - 1, D_MODEL), F32)], axis=0)
    mods = _ada_rows(cvec, w_ada, b_ada)
    rope_tabs = _rope_tables(x_sample.shape[1])
    n_past = cache_k.shape[2]
    y_p, y_s = x_prompt, x_sample
    new_k, new_v = [], []
    for l in range(DEPTH):
        row = lambda a: a[l].reshape(1, -1)
        w = {
            "in": w_in[l].astype(BF16), "q_norm_g": row(q_norm_g), "k_norm_g": row(k_norm_g),
            "pool_g": w_pool_g[l].astype(BF16), "pool_scale": row(pool_scale),
            "sgu": w_sgu[l].astype(BF16), "b_sgu": b_sgu[l].T,
            "attn_o": w_attn_o[l].astype(BF16), "pool_o": w_pool_o[l].astype(BF16),
            "sgu_o": w_sgu_o[l].astype(BF16), "out": w_out[l].astype(BF16),
            "ln1_g": row(ln1_g), "ln1_b": row(ln1_b),
            "ffn_a": w_ffn_in[l, :, :D_FF].astype(BF16), "ffn_b": w_ffn_in[l, :, D_FF:].astype(BF16),
            "ffn_o": w_ffn_out[l].astype(BF16), "ln2_g": row(ln2_g), "ln2_b": row(ln2_b),
        }
        mod_lat = mods[l, :n_dec].reshape(n_dec, 1, 6 * D_MODEL)
        mod_ctx = mods[l, n_dec:n_dec + 1].reshape(1, 1, 6 * D_MODEL)
        y_p, (k_ctx, v_ctx) = _trunk_layer(y_p, mod_ctx, None, None, None, w, x_prompt.shape[1], True)
        new_k.append(k_ctx)
        new_v.append(v_ctx)
        ctx_k = cache_k[:, l].reshape(n_dec, n_past, KV_WIDTH)
        ctx_v = cache_v[:, l].reshape(n_dec, n_past, KV_WIDTH)
        y_s, _ = _trunk_layer(y_s, mod_lat, rope_tabs, ctx_k, ctx_v, w, TOKEN_TILE, False)
    cache_shape = (x_prompt.shape[0], DEPTH, x_prompt.shape[1], N_KV_HEADS, HEAD_DIM)
    new_cache_k = jnp.stack(new_k, axis=1).reshape(cache_shape)
    new_cache_v = jnp.stack(new_v, axis=1).reshape(cache_shape)
    return (y_p, y_s, new_cache_k, new_cache_v)
```

```python
import functools
import math

import jax
import jax.numpy as jnp
import numpy as np
from jax import lax
from jax.experimental import pallas as pl
from jax.experimental.pallas import tpu as pltpu

D_MODEL = 1024
DEPTH = 2
GRID_W = 64
HEAD_DIM = 128
N_Q_HEADS = 8
N_KV_HEADS = 2
Q_PER_KV = N_Q_HEADS // N_KV_HEADS
ATTN_WIDTH = N_Q_HEADS * HEAD_DIM
KV_WIDTH = N_KV_HEADS * HEAD_DIM
ROPE_THETA = 10000.0
ROPE_PAIRS = HEAD_DIM // 4
POOL_WINDOWS = (2, 4, 8, 16)
N_POOL_GROUPS = 4
POOL_WIDTH = D_MODEL // 2
POOL_GROUP_DIM = POOL_WIDTH // N_POOL_GROUPS
POOL_HALO = max(POOL_WINDOWS) // 2
CHUNK = 128
N_SGU_GROUPS = 4
SGU_WIDTH = D_MODEL // 2
SGU_GROUP_DIM = SGU_WIDTH // N_SGU_GROUPS
N_BRANCHES = 3
IN_WIDTH = ATTN_WIDTH + 2 * KV_WIDTH + POOL_WIDTH + 2 * SGU_WIDTH + N_BRANCHES * D_MODEL
SPLIT_Q = ATTN_WIDTH
SPLIT_K = SPLIT_Q + KV_WIDTH
SPLIT_V = SPLIT_K + KV_WIDTH
SPLIT_POOL = SPLIT_V + POOL_WIDTH
SPLIT_U = SPLIT_POOL + SGU_WIDTH
SPLIT_SV = SPLIT_U + SGU_WIDTH
D_FF = ((8 * D_MODEL + 3 * 256 - 1) // (3 * 256)) * 256
DEEPNORM_ALPHA = (2 * DEPTH) ** 0.25
EPS = 1e-6

Q_PRESCALE = HEAD_DIM ** -0.5 * math.log2(math.e)
SQRT_HALF = 0.5 ** 0.5

V7X_VMEM_LIMIT_BYTES = 56 * 1024 * 1024
ADA_ROWS = 16
ADA_TN = 1536
TOKEN_TILE = 512
FFN_TOKEN_TILE = 1024
MIX_TOKEN_TILE = 512
ATTN_Q_TILE = 512
ATTN_KV_CHUNK = 768
ATTN_HEADS_PER_UNIT = 1
TAIL_ROW_BLOCKS = 2
FFN_CHUNK_EDGES = (0, 1280, D_FF)

BF16 = jnp.bfloat16
F32 = jnp.float32


def _dot(a, b):
    return jnp.dot(a, b, preferred_element_type=F32)


def _resident(shape, layer):
    nd = len(shape)
    return pl.BlockSpec((None,) + tuple(shape), lambda *_: (layer,) + (0,) * nd, pipeline_mode=pl.Buffered(1))


def _mod_spec(layer, mod_row):
    if mod_row is None:
        imap = lambda bi, i: (layer, bi, 0, 0)
    else:
        imap = lambda bi, i: (layer, mod_row, 0, 0)
    return pl.BlockSpec((None, None, 1, 6 * D_MODEL), imap)


def _params(n_grid):
    return pltpu.CompilerParams(dimension_semantics=("arbitrary",) * n_grid,
                                vmem_limit_bytes=V7X_VMEM_LIMIT_BYTES)


def _layer_norm(y, g, b):
    mu = jnp.mean(y, axis=-1, keepdims=True)
    yc = y - mu
    var = jnp.mean(yc * yc, axis=-1, keepdims=True)
    return yc * lax.rsqrt(var + EPS) * g + b


def _gelu(x):
    return 0.5 * x * (1.0 + lax.erf(x * SQRT_HALF))


def _sigmoid(x):
    return 0.5 * jnp.tanh(0.5 * x) + 0.5


def _ada_kernel(c_ref, w_ref, b_ref, o_ref):
    c = c_ref[...]
    s = c * jax.nn.sigmoid(c)
    o_ref[0] = _dot(s.astype(BF16), w_ref[0].astype(BF16)) + b_ref[0]


def _ada_rows(cvec, w_ada, b_ada):
    n = 6 * D_MODEL
    return pl.pallas_call(
        _ada_kernel,
        grid=(DEPTH, n // ADA_TN),
        in_specs=[pl.BlockSpec((ADA_ROWS, D_MODEL), lambda l, j: (0, 0)),
                  pl.BlockSpec((1, D_MODEL, ADA_TN), lambda l, j: (l, 0, j)),
                  pl.BlockSpec((1, 1, ADA_TN), lambda l, j: (l, 0, j))],
        out_specs=pl.BlockSpec((1, ADA_ROWS, ADA_TN), lambda l, j: (l, 0, j)),
        out_shape=jax.ShapeDtypeStruct((DEPTH, ADA_ROWS, n), F32),
        compiler_params=_params(2),
        name="ada_rows",
    )(cvec, w_ada, b_ada.reshape(DEPTH, 1, n))


def _inproj_kernel(rope, emit_cache, x_ref, mod_ref, w_ref, qg_ref, kg_ref, *refs):
    refs = list(refs)
    if rope:
        cos_ref, sina_ref, sinb_ref = refs[:3]
        refs = refs[3:]
    q_ref, k_ref, vt_ref, xp_ref, u_ref, vn_ref, g_ref = refs[:7]
    if emit_cache:
        kc_ref, vc_ref = refs[7:9]

    x = x_ref[0]
    mod = mod_ref[...]
    sh1 = mod[:, 0:D_MODEL]
    sc1 = mod[:, D_MODEL:2 * D_MODEL]
    h = (x * (1.0 + sc1) + sh1).astype(BF16)

    def proj(lo, hi):
        return _dot(h, w_ref[:, lo:hi])

    def rms(t, g):
        return t * lax.rsqrt(jnp.mean(t * t, axis=-1, keepdims=True) + EPS) * g

    def rotate(t):
        return (t * cos_ref[...] + pltpu.roll(t, HEAD_DIM - ROPE_PAIRS, 1) * sina_ref[...]
                + pltpu.roll(t, ROPE_PAIRS, 1) * sinb_ref[...])

    u_ref[0] = _gelu(proj(SPLIT_POOL, SPLIT_U)).astype(BF16)
    gv = _gelu(proj(SPLIT_U, SPLIT_SV))
    mu = jnp.mean(gv, axis=-1, keepdims=True)
    gc = gv - mu
    var = jnp.mean(gc * gc, axis=-1, keepdims=True)
    vn_ref[0] = (gc * lax.rsqrt(var + EPS)).astype(BF16)

    qp = proj(0, SPLIT_Q)
    qg = qg_ref[...]
    for hd in range(N_Q_HEADS):
        sl = slice(hd * HEAD_DIM, (hd + 1) * HEAD_DIM)
        t = rms(qp[:, sl], qg)
        if rope:
            t = rotate(t)
        q_ref[0, hd] = (t * Q_PRESCALE).astype(BF16)

    kv = proj(SPLIT_Q, SPLIT_V)
    kg = kg_ref[...]
    for hd in range(N_KV_HEADS):
        sl = slice(hd * HEAD_DIM, (hd + 1) * HEAD_DIM)
        t = rms(kv[:, sl], kg)
        if emit_cache:
            kc_ref[0, :, sl] = t
        if rope:
            t = rotate(t)
        k_ref[0, :, sl] = t.astype(BF16)
    v = kv[:, KV_WIDTH:]
    if emit_cache:
        vc_ref[0] = v
    vt_ref[0] = v.T.astype(BF16)

    for br in range(N_BRANCHES):
        lo = SPLIT_SV + br * D_MODEL
        g_ref[0, :, br * D_MODEL:(br + 1) * D_MODEL] = _sigmoid(proj(lo, lo + D_MODEL)).astype(BF16)

    xp_ref[0] = proj(SPLIT_V, SPLIT_POOL)


def _inproj(x, mod, mod_row, w, layer, rope_tabs, tm, emit_cache):
    b, s, _ = x.shape
    rope = rope_tabs is not None
    tok = lambda wd: pl.BlockSpec((1, tm, wd), lambda bi, i: (bi, i, 0))
    in_specs = [tok(D_MODEL),
                _mod_spec(layer, mod_row),
                _resident((D_MODEL, IN_WIDTH), layer), _resident((1, HEAD_DIM), layer), _resident((1, HEAD_DIM), layer)]
    args = [x, mod, w["in"], w["q_norm_g"], w["k_norm_g"]]
    if rope:
        in_specs += [pl.BlockSpec((tm, HEAD_DIM), lambda bi, i: (i, 0))] * 3
        args += list(rope_tabs)
    out_specs = [pl.BlockSpec((1, N_Q_HEADS, tm, HEAD_DIM), lambda bi, i: (bi, 0, i, 0)), tok(KV_WIDTH),
                 pl.BlockSpec((1, KV_WIDTH, tm), lambda bi, i: (bi, 0, i)),
                 tok(POOL_WIDTH), tok(SGU_WIDTH), tok(SGU_WIDTH), tok(N_BRANCHES * D_MODEL)]
    sds = lambda wd, dt: jax.ShapeDtypeStruct((b, s, wd), dt)
    out_shape = [jax.ShapeDtypeStruct((b, N_Q_HEADS, s, HEAD_DIM), BF16), sds(KV_WIDTH, BF16),
                 jax.ShapeDtypeStruct((b, KV_WIDTH, s), BF16),
                 sds(POOL_WIDTH, F32), sds(SGU_WIDTH, BF16), sds(SGU_WIDTH, BF16), sds(N_BRANCHES * D_MODEL, BF16)]
    if emit_cache:
        out_specs += [tok(KV_WIDTH), tok(KV_WIDTH)]
        out_shape += [sds(KV_WIDTH, F32), sds(KV_WIDTH, F32)]
    return pl.pallas_call(
        functools.partial(_inproj_kernel, rope, emit_cache),
        grid=(b, s // tm), in_specs=in_specs, out_specs=out_specs, out_shape=out_shape,
        compiler_params=_params(2), name="inproj_rope" if rope else "inproj_ctx",
    )(*args)


def _attn_kernel(n_ctx, n_chunks, tkc, lookahead, q_ref, *refs):
    refs = list(refs)
    qn_ref = refs.pop(0) if lookahead else None
    k_ref, vt_ref = refs[:2]
    refs = refs[2:]
    if n_ctx:
        ck_ref, cv_ref = refs[:2]
        refs = refs[2:]
    o_ref, kall, vtall, st_scr, cm_scr, m_scr, l_scr, acc_scr = refs
    tq = q_ref.shape[2]
    n_units = m_scr.shape[0]
    hpu = N_Q_HEADS // n_units

    def scores(src_ref, c, slot, u):
        g = u * hpu // Q_PER_KV
        kc = kall[pl.ds(pl.multiple_of(c * tkc, tkc), tkc), g * HEAD_DIM:(g + 1) * HEAD_DIM]
        qu = src_ref[0, u * hpu:(u + 1) * hpu].reshape(hpu * tq, HEAD_DIM)
        st = lax.dot_general(kc, qu, (((1,), (1,)), ((), ())), preferred_element_type=F32)
        st_scr[slot, u] = st
        cm_scr[slot, u] = jnp.max(st, axis=0, keepdims=True)

    def accumulate(c, slot, u):
        g = u * hpu // Q_PER_KV
        m_old = m_scr[u]
        m_new = jnp.maximum(m_old, cm_scr[slot, u])
        p = jnp.exp2(st_scr[slot, u] - m_new)
        alpha = jnp.exp2(m_old - m_new)
        l_scr[u] = alpha * l_scr[u] + jnp.sum(p, axis=0, keepdims=True)
        vtc = vtall[c, g * HEAD_DIM:(g + 1) * HEAD_DIM, :]
        acc_scr[u] = alpha * acc_scr[u] + _dot(vtc, p.astype(BF16))
        m_scr[u] = m_new

    def finalize(u):
        ot = acc_scr[u] / l_scr[u]
        for j in range(hpu):
            hd = u * hpu + j
            o_ref[0, :, hd * HEAD_DIM:(hd + 1) * HEAD_DIM] = ot[:, j * tq:(j + 1) * tq].T.astype(BF16)

    def first_scores():
        for u in range(n_units):
            scores(q_ref, 0, 0, u)

    @pl.when(pl.program_id(1) == 0)
    def _stage_keys():
        if n_ctx:
            kall[0:n_ctx, :] = ck_ref[0].astype(BF16)
            cvt = cv_ref[0].T.astype(BF16)
        kall[n_ctx:, :] = k_ref[0]
        for c in range(n_chunks):
            lo, hi = c * tkc, (c + 1) * tkc
            if lo < n_ctx:
                end = min(hi, n_ctx)
                vtall[c, :, 0:end - lo] = cvt[:, lo:end]
            if hi > n_ctx:
                beg = max(lo, n_ctx)
                vtall[c, :, beg - lo:tkc] = vt_ref[0, :, beg - n_ctx:hi - n_ctx]
        if lookahead:
            first_scores()

    if not lookahead:
        first_scores()
    m_scr[...] = jnp.full(m_scr.shape, -jnp.inf, F32)
    l_scr[...] = jnp.zeros(l_scr.shape, F32)
    acc_scr[...] = jnp.zeros(acc_scr.shape, F32)

    def step(c, slot, last):
        for u in range(n_units):
            if not last:
                scores(q_ref, c + 1, 1 - slot, u)
            elif lookahead:
                scores(qn_ref, 0, 1 - slot, u)
            accumulate(c, slot, u)
            if last:
                finalize(u)

    n_tail = 2 - n_chunks % 2
    n_pairs = (n_chunks - n_tail) // 2

    def pair_step(j, carry):
        step(2 * j, 0, False)
        step(2 * j + 1, 1, False)
        return carry

    lax.fori_loop(0, n_pairs, pair_step, 0)
    for c in range(2 * n_pairs, n_chunks):
        step(c, c % 2, c == n_chunks - 1)


def _attention(q, k, vt, ctx_k, ctx_v, layer, tq, tkc):
    b, _, s, _ = q.shape
    n_ctx = 0 if ctx_k is None else ctx_k.shape[2]
    total = n_ctx + s
    n_chunks = total // tkc
    n_q = s // tq
    lookahead = n_q > 1 and n_chunks % 2 == 0
    q_spec = lambda imap: pl.BlockSpec((1, N_Q_HEADS, tq, HEAD_DIM), imap)
    in_specs = [q_spec(lambda bi, i: (bi, 0, i, 0))]
    args = [q]
    if lookahead:
        in_specs.append(q_spec(lambda bi, i: (bi, 0, jnp.minimum(i + 1, n_q - 1), 0)))
        args.append(q)
    in_specs += [pl.BlockSpec((1, s, KV_WIDTH), lambda bi, i: (bi, 0, 0)),
                 pl.BlockSpec((1, KV_WIDTH, s), lambda bi, i: (bi, 0, 0))]
    args += [k, vt]
    if n_ctx:
        in_specs += [pl.BlockSpec((1, None, n_ctx, KV_WIDTH), lambda bi, i: (bi, layer, 0, 0))] * 2
        args += [ctx_k, ctx_v]
    n_units = N_Q_HEADS // ATTN_HEADS_PER_UNIT
    lanes = ATTN_HEADS_PER_UNIT * tq
    return pl.pallas_call(
        functools.partial(_attn_kernel, n_ctx, n_chunks, tkc, lookahead),
        grid=(b, n_q), in_specs=in_specs,
        out_specs=pl.BlockSpec((1, tq, ATTN_WIDTH), lambda bi, i: (bi, i, 0)),
        out_shape=jax.ShapeDtypeStruct((b, s, ATTN_WIDTH), BF16),
        scratch_shapes=[pltpu.VMEM((total, KV_WIDTH), BF16),
                        pltpu.VMEM((n_chunks, KV_WIDTH, tkc), BF16),
                        pltpu.VMEM((2, n_units, tkc, lanes), F32),
                        pltpu.VMEM((2, n_units, 1, lanes), F32),
                        pltpu.VMEM((n_units, 1, lanes), F32),
                        pltpu.VMEM((n_units, 1, lanes), F32),
                        pltpu.VMEM((n_units, HEAD_DIM, lanes), F32)],
        compiler_params=_params(2), name="attn_lat" if n_ctx else "attn_ctx",
    )(*args)


def _mix_kernel(x_ref, mod_ref, attn_ref, xp_ref, xprev_ref, xnext_ref, cnt_ref, u_ref, vn_ref, g_ref,
                wao_ref, wpo_ref, wso_ref, wout_ref, wpg_ref, psc_ref, ws_ref, bs_ref, lng_ref, lnb_ref,
                o_ref, pool_scr, sgu_scr):
    tm = x_ref.shape[1]
    i = pl.program_id(1)
    nt = pl.num_programs(1)
    xc = xp_ref[0]
    xe = jnp.concatenate([jnp.where(i > 0, xprev_ref[0], 0.0), xc, jnp.where(i < nt - 1, xnext_ref[0], 0.0)], axis=0)
    n_rows = tm + 2 * POOL_HALO
    up = lambda a, s: pltpu.roll(a, n_rows - s, 0)
    n_c = tm // CHUNK
    attn = attn_ref[0]
    cols = D_MODEL // N_POOL_GROUPS
    attn_parts = []
    for g, w in enumerate(POOL_WINDOWS):
        gl = slice(g * POOL_GROUP_DIM, (g + 1) * POOL_GROUP_DIM)
        acc = xe[:, gl]
        span = 1
        while span < w:
            acc = acc + up(acc, span)
            span *= 2
        first = POOL_HALO - w // 2
        sums = (up(acc, first) if first else acc)[0:tm]
        pooled = (sums / cnt_ref[:, gl] - xc[:, gl]).astype(BF16)
        pool_scr[:, gl] = (_dot(pooled, wpg_ref[g]) * psc_ref[:, gl]).astype(BF16)

        cs = slice(g * cols, (g + 1) * cols)
        attn_parts.append(g_ref[0, :, cs].astype(F32) * _dot(attn, wao_ref[:, cs]))

        vcat = jnp.concatenate([vn_ref[0, c * CHUNK:(c + 1) * CHUNK, gl] for c in range(n_c)], axis=1)
        mixed = _dot(ws_ref[g], vcat) + bs_ref[:, g:g + 1]
        for c in range(n_c):
            rows = slice(c * CHUNK, (c + 1) * CHUNK)
            sgu_scr[rows, gl] = (u_ref[0, rows, gl].astype(F32) * mixed[:, c * CHUNK:(c + 1) * CHUNK]).astype(BF16)

    merged = jnp.concatenate(attn_parts, axis=1)
    merged = merged + g_ref[0, :, D_MODEL:2 * D_MODEL].astype(F32) * _dot(pool_scr[...], wpo_ref[...])
    merged = merged + g_ref[0, :, 2 * D_MODEL:].astype(F32) * _dot(sgu_scr[...], wso_ref[...])
    merged = merged.astype(BF16)
    g1 = mod_ref[:, 2 * D_MODEL:3 * D_MODEL]
    rb = tm // TAIL_ROW_BLOCKS
    for r in range(TAIL_ROW_BLOCKS):
        rows = slice(r * rb, (r + 1) * rb)
        mix = _dot(merged[rows], wout_ref[...])
        o_ref[0, rows, :] = _layer_norm(DEEPNORM_ALPHA * x_ref[0, rows, :] + g1 * mix, lng_ref[...], lnb_ref[...])


def _pool_counts(seq):
    t = np.arange(seq, dtype=np.int32)[:, None]
    w = np.repeat(np.array(POOL_WINDOWS, dtype=np.int32), POOL_GROUP_DIM)[None, :]
    return jnp.asarray((np.minimum(t + (w - w // 2), seq) - np.maximum(t - w // 2, 0)).astype(np.float32))


def _mix(x, mod, mod_row, attn, xp, u, vn, gates, w, layer, tm):
    b, s, _ = x.shape
    hb = tm // POOL_HALO
    last_halo = s // POOL_HALO - 1
    tok = lambda wd: pl.BlockSpec((1, tm, wd), lambda bi, i: (bi, i, 0))
    in_specs = [tok(D_MODEL),
                _mod_spec(layer, mod_row),
                tok(ATTN_WIDTH), tok(POOL_WIDTH),
                pl.BlockSpec((1, POOL_HALO, POOL_WIDTH), lambda bi, i: (bi, jnp.maximum(i * hb - 1, 0), 0)),
                pl.BlockSpec((1, POOL_HALO, POOL_WIDTH), lambda bi, i: (bi, jnp.minimum((i + 1) * hb, last_halo), 0)),
                pl.BlockSpec((tm, POOL_WIDTH), lambda bi, i: (i, 0)),
                tok(SGU_WIDTH), tok(SGU_WIDTH), tok(N_BRANCHES * D_MODEL),
                *[_resident(shape, layer) for shape in (
                    (ATTN_WIDTH, D_MODEL), (POOL_WIDTH, D_MODEL), (SGU_WIDTH, D_MODEL), (D_MODEL, D_MODEL),
                    (N_POOL_GROUPS, POOL_GROUP_DIM, POOL_GROUP_DIM), (1, POOL_WIDTH), (N_SGU_GROUPS, CHUNK, CHUNK),
                    (CHUNK, N_SGU_GROUPS), (1, D_MODEL), (1, D_MODEL))]]
    return pl.pallas_call(
        _mix_kernel,
        grid=(b, s // tm), in_specs=in_specs, out_specs=tok(D_MODEL),
        out_shape=jax.ShapeDtypeStruct((b, s, D_MODEL), F32),
        scratch_shapes=[pltpu.VMEM((tm, POOL_WIDTH), BF16), pltpu.VMEM((tm, SGU_WIDTH), BF16)],
        compiler_params=_params(2), name="mix",
    )(x, mod, attn, xp, xp, xp, _pool_counts(s), u, vn, gates, w["attn_o"], w["pool_o"], w["sgu_o"], w["out"], w["pool_g"],
      w["pool_scale"], w["sgu"], w["b_sgu"], w["ln1_g"], w["ln1_b"])


def _ffn_kernel(x_ref, mod_ref, win_ref, wo_ref, lng_ref, lnb_ref, o_ref):
    x = x_ref[0]
    tm = x.shape[0]
    mod = mod_ref[...]
    sh2 = mod[:, 3 * D_MODEL:4 * D_MODEL]
    sc2 = mod[:, 4 * D_MODEL:5 * D_MODEL]
    g2 = mod[:, 5 * D_MODEL:6 * D_MODEL]
    h = (x * (1.0 + sc2) + sh2).astype(BF16)

    def hidden(lo, hi):
        a = _dot(h, win_ref[:, lo:hi])
        bgate = _dot(h, win_ref[:, D_FF + lo:D_FF + hi])
        return (a * _sigmoid(a) * bgate).astype(BF16)

    edges = list(zip(FFN_CHUNK_EDGES[:-1], FFN_CHUNK_EDGES[1:]))
    f = None
    for lo, hi in edges[:-1]:
        part = _dot(hidden(lo, hi), wo_ref[lo:hi, :])
        f = part if f is None else f + part
    lo, hi = edges[-1]
    act = hidden(lo, hi)
    rb = tm // TAIL_ROW_BLOCKS
    for r in range(TAIL_ROW_BLOCKS):
        rows = slice(r * rb, (r + 1) * rb)
        part = _dot(act[rows], wo_ref[lo:hi, :])
        fr = part if f is None else f[rows] + part
        o_ref[0, rows, :] = _layer_norm(DEEPNORM_ALPHA * x[rows] + g2 * fr, lng_ref[...], lnb_ref[...])


def _ffn(x, mod, mod_row, w, layer, tm):
    b, s, _ = x.shape
    tok = pl.BlockSpec((1, tm, D_MODEL), lambda bi, i: (bi, i, 0))
    in_specs = [tok,
                _mod_spec(layer, mod_row),
                _resident((D_MODEL, 2 * D_FF), layer), _resident((D_FF, D_MODEL), layer),
                _resident((1, D_MODEL), layer), _resident((1, D_MODEL), layer)]
    return pl.pallas_call(
        _ffn_kernel, grid=(b, s // tm), in_specs=in_specs, out_specs=tok,
        out_shape=jax.ShapeDtypeStruct((b, s, D_MODEL), F32),
        compiler_params=_params(2), name="ffn",
    )(x, mod, w["ffn_in"], w["ffn_o"], w["ln2_g"], w["ln2_b"])


def _rope_tables(seq):
    rows = seq // GRID_W
    row = jnp.repeat(jnp.arange(rows), GRID_W).astype(F32)
    col = jnp.tile(jnp.arange(GRID_W), rows).astype(F32)
    inv_freq = ROPE_THETA ** (-jnp.arange(ROPE_PAIRS, dtype=F32) / ROPE_PAIRS)
    ar = row[:, None] * inv_freq
    ac = col[:, None] * inv_freq
    z = jnp.zeros_like(ar)
    cos = jnp.concatenate([jnp.cos(ar), jnp.cos(ar), jnp.cos(ac), jnp.cos(ac)], axis=1)
    sina = jnp.concatenate([-jnp.sin(ar), z, -jnp.sin(ac), z], axis=1)
    sinb = jnp.concatenate([z, jnp.sin(ar), z, jnp.sin(ac)], axis=1)
    return cos, sina, sinb


def _trunk_layer(x, mod, mod_row, rope_tabs, ctx_k, ctx_v, w, layer, tm, emit_cache):
    outs = _inproj(x, mod, mod_row, w, layer, rope_tabs, tm, emit_cache)
    q, k, vt, xp, u, vn, gates = outs[:7]
    s = x.shape[1]
    attn = _attention(q, k, vt, ctx_k, ctx_v, layer, min(ATTN_Q_TILE, s), min(ATTN_KV_CHUNK, s))
    x = _mix(x, mod, mod_row, attn, xp, u, vn, gates, w, layer, min(MIX_TOKEN_TILE, s))
    if mod_row is None:
        x = _ffn(x, mod, mod_row, w, layer, min(FFN_TOKEN_TILE, s))
    else:
        b = x.shape[0]
        x = _ffn(x.reshape(1, b * s, D_MODEL), mod, mod_row, w, layer, min(FFN_TOKEN_TILE, b * s)).reshape(b, s, D_MODEL)
    return x, outs[7:]


def kernel(x_prompt, x_sample, cache_k, cache_v, c, c_ctx, w_ada, b_ada, w_in, q_norm_g, k_norm_g, w_pool_g,
           pool_scale, w_sgu, b_sgu, w_attn_o, w_pool_o, w_sgu_o, w_out, ln1_g, ln1_b, w_ffn_in, w_ffn_out,
           ln2_g, ln2_b):
    n_dec = c.shape[0]
    cvec = jnp.concatenate([c, c_ctx[None, :], jnp.zeros((ADA_ROWS - n_dec - 1, D_MODEL), F32)], axis=0)
    mods = _ada_rows(cvec, w_ada, b_ada).reshape(DEPTH, ADA_ROWS, 1, 6 * D_MODEL)
    rope_tabs = _rope_tables(x_sample.shape[1])
    n_past = cache_k.shape[2]
    ctx_k = cache_k.reshape(n_dec, DEPTH, n_past, KV_WIDTH)
    ctx_v = cache_v.reshape(n_dec, DEPTH, n_past, KV_WIDTH)
    rows = lambda a: a.reshape(DEPTH, 1, -1)
    w = {
        "in": w_in.astype(BF16), "q_norm_g": rows(q_norm_g), "k_norm_g": rows(k_norm_g),
        "pool_g": w_pool_g.astype(BF16), "pool_scale": rows(pool_scale),
        "sgu": w_sgu.astype(BF16), "b_sgu": jnp.swapaxes(b_sgu, 1, 2),
        "attn_o": w_attn_o.astype(BF16), "pool_o": w_pool_o.astype(BF16),
        "sgu_o": w_sgu_o.astype(BF16), "out": w_out.astype(BF16),
        "ln1_g": rows(ln1_g), "ln1_b": rows(ln1_b),
        "ffn_in": w_ffn_in.astype(BF16), "ffn_o": w_ffn_out.astype(BF16),
        "ln2_g": rows(ln2_g), "ln2_b": rows(ln2_b),
    }
    y_p, y_s = x_prompt, x_sample
    new_k, new_v = [], []
    for l in range(DEPTH):
        y_p, (k_ctx, v_ctx) = _trunk_layer(y_p, mods, n_dec, None, None, None, w, l, x_prompt.shape[1], True)
        new_k.append(k_ctx)
        new_v.append(v_ctx)
        y_s, _ = _trunk_layer(y_s, mods, None, rope_tabs, ctx_k, ctx_v, w, l, TOKEN_TILE, False)
    cache_shape = (x_prompt.shape[0], DEPTH, x_prompt.shape[1], N_KV_HEADS, HEAD_DIM)
    new_cache_k = jnp.stack(new_k, axis=1).reshape(cache_shape)
    new_cache_v = jnp.stack(new_v, axis=1).reshape(cache_shape)
    return (y_p, y_s, new_cache_k, new_cache_v)
```

```python
import functools
import math

import jax
import jax.numpy as jnp
import numpy as np
from jax import lax
from jax.experimental import pallas as pl
from jax.experimental.pallas import tpu as pltpu

D_MODEL = 1024
DEPTH = 2
GRID_W = 64
HEAD_DIM = 128
N_Q_HEADS = 8
N_KV_HEADS = 2
Q_PER_KV = N_Q_HEADS // N_KV_HEADS
ATTN_WIDTH = N_Q_HEADS * HEAD_DIM
KV_WIDTH = N_KV_HEADS * HEAD_DIM
ROPE_THETA = 10000.0
ROPE_PAIRS = HEAD_DIM // 4
POOL_WINDOWS = (2, 4, 8, 16)
N_POOL_GROUPS = 4
POOL_WIDTH = D_MODEL // 2
POOL_GROUP_DIM = POOL_WIDTH // N_POOL_GROUPS
POOL_HALO = max(POOL_WINDOWS) // 2
CHUNK = 128
N_SGU_GROUPS = 4
SGU_WIDTH = D_MODEL // 2
SGU_GROUP_DIM = SGU_WIDTH // N_SGU_GROUPS
N_BRANCHES = 3
IN_WIDTH = ATTN_WIDTH + 2 * KV_WIDTH + POOL_WIDTH + 2 * SGU_WIDTH + N_BRANCHES * D_MODEL
SPLIT_Q = ATTN_WIDTH
SPLIT_K = SPLIT_Q + KV_WIDTH
SPLIT_V = SPLIT_K + KV_WIDTH
SPLIT_POOL = SPLIT_V + POOL_WIDTH
SPLIT_U = SPLIT_POOL + SGU_WIDTH
SPLIT_SV = SPLIT_U + SGU_WIDTH
D_FF = ((8 * D_MODEL + 3 * 256 - 1) // (3 * 256)) * 256
DEEPNORM_ALPHA = (2 * DEPTH) ** 0.25
EPS = 1e-6

Q_PRESCALE = HEAD_DIM ** -0.5 * math.log2(math.e)
SQRT_HALF = 0.5 ** 0.5

V7X_VMEM_LIMIT_BYTES = 56 * 1024 * 1024
ADA_ROWS = 16
ADA_TN = 1536
TOKEN_TILE = 512
FFN_TOKEN_TILE = 1024
MIX_TOKEN_TILE = 512
ATTN_Q_TILE = 512
ATTN_KV_CHUNK = 768
ATTN_HEADS_PER_UNIT = 1
TAIL_ROW_BLOCKS = 2
FFN_CHUNK_EDGES = (0, 1280, D_FF)

BF16 = jnp.bfloat16
F32 = jnp.float32


def _dot(a, b):
    return jnp.dot(a, b, preferred_element_type=F32)


def _resident(shape, layer):
    nd = len(shape)
    return pl.BlockSpec((None,) + tuple(shape), lambda *_: (layer,) + (0,) * nd, pipeline_mode=pl.Buffered(1))


def _mod_spec(layer, mod_row):
    if mod_row is None:
        imap = lambda bi, i: (layer, bi, 0, 0)
    else:
        imap = lambda bi, i: (layer, mod_row, 0, 0)
    return pl.BlockSpec((None, None, 1, 6 * D_MODEL), imap)


def _params(n_grid):
    return pltpu.CompilerParams(dimension_semantics=("arbitrary",) * n_grid,
                                vmem_limit_bytes=V7X_VMEM_LIMIT_BYTES)


def _layer_norm(y, g, b):
    mu = jnp.mean(y, axis=-1, keepdims=True)
    yc = y - mu
    var = jnp.mean(yc * yc, axis=-1, keepdims=True)
    return yc * lax.rsqrt(var + EPS) * g + b


def _gelu(x):
    return 0.5 * x * (1.0 + lax.erf(x * SQRT_HALF))


def _sigmoid(x):
    return 0.5 * jnp.tanh(0.5 * x) + 0.5


def _ada_kernel(c_ref, w_ref, b_ref, o_ref):
    c = c_ref[...]
    s = c * jax.nn.sigmoid(c)
    o_ref[0] = _dot(s.astype(BF16), w_ref[0].astype(BF16)) + b_ref[0]


def _ada_rows(cvec, w_ada, b_ada):
    n = 6 * D_MODEL
    return pl.pallas_call(
        _ada_kernel,
        grid=(DEPTH, n // ADA_TN),
        in_specs=[pl.BlockSpec((ADA_ROWS, D_MODEL), lambda l, j: (0, 0)),
                  pl.BlockSpec((1, D_MODEL, ADA_TN), lambda l, j: (l, 0, j)),
                  pl.BlockSpec((1, 1, ADA_TN), lambda l, j: (l, 0, j))],
        out_specs=pl.BlockSpec((1, ADA_ROWS, ADA_TN), lambda l, j: (l, 0, j)),
        out_shape=jax.ShapeDtypeStruct((DEPTH, ADA_ROWS, n), F32),
        compiler_params=_params(2),
        name="ada_rows",
    )(cvec, w_ada, b_ada.reshape(DEPTH, 1, n))


def _inproj_kernel(rope, emit_cache, x_ref, mod_ref, w_ref, qg_ref, kg_ref, *refs):
    refs = list(refs)
    if rope:
        cos_ref, sina_ref, sinb_ref = refs[:3]
        refs = refs[3:]
    q_ref, k_ref, vt_ref, xp_ref, u_ref, vn_ref, g_ref = refs[:7]
    if emit_cache:
        kc_ref, vc_ref = refs[7:9]

    x = x_ref[0]
    mod = mod_ref[...]
    sh1 = mod[:, 0:D_MODEL]
    sc1 = mod[:, D_MODEL:2 * D_MODEL]
    h = (x * (1.0 + sc1) + sh1).astype(BF16)

    def proj(lo, hi):
        return _dot(h, w_ref[:, lo:hi])

    def rms(t, g):
        return t * lax.rsqrt(jnp.mean(t * t, axis=-1, keepdims=True) + EPS) * g

    def rotate(t):
        return (t * cos_ref[...] + pltpu.roll(t, HEAD_DIM - ROPE_PAIRS, 1) * sina_ref[...]
                + pltpu.roll(t, ROPE_PAIRS, 1) * sinb_ref[...])

    u_ref[0] = _gelu(proj(SPLIT_POOL, SPLIT_U)).astype(BF16)
    gv = _gelu(proj(SPLIT_U, SPLIT_SV))
    mu = jnp.mean(gv, axis=-1, keepdims=True)
    gc = gv - mu
    var = jnp.mean(gc * gc, axis=-1, keepdims=True)
    vn_ref[0] = (gc * lax.rsqrt(var + EPS)).astype(BF16)

    qp = proj(0, SPLIT_Q)
    qg = qg_ref[...]
    for hd in range(N_Q_HEADS):
        sl = slice(hd * HEAD_DIM, (hd + 1) * HEAD_DIM)
        t = rms(qp[:, sl], qg)
        if rope:
            t = rotate(t)
        q_ref[0, hd] = (t * Q_PRESCALE).astype(BF16)

    kv = proj(SPLIT_Q, SPLIT_V)
    kg = kg_ref[...]
    for hd in range(N_KV_HEADS):
        sl = slice(hd * HEAD_DIM, (hd + 1) * HEAD_DIM)
        t = rms(kv[:, sl], kg)
        if emit_cache:
            kc_ref[0, :, sl] = t
        if rope:
            t = rotate(t)
        k_ref[0, :, sl] = t.astype(BF16)
    v = kv[:, KV_WIDTH:]
    if emit_cache:
        vc_ref[0] = v
    vt_ref[0] = v.T.astype(BF16)

    for br in range(N_BRANCHES):
        lo = SPLIT_SV + br * D_MODEL
        g_ref[0, :, br * D_MODEL:(br + 1) * D_MODEL] = _sigmoid(proj(lo, lo + D_MODEL)).astype(BF16)

    xp_ref[0] = proj(SPLIT_V, SPLIT_POOL)


def _inproj(x, mod, mod_row, w, layer, rope_tabs, tm, emit_cache):
    b, s, _ = x.shape
    rope = rope_tabs is not None
    tok = lambda wd: pl.BlockSpec((1, tm, wd), lambda bi, i: (bi, i, 0))
    in_specs = [tok(D_MODEL),
                _mod_spec(layer, mod_row),
                _resident((D_MODEL, IN_WIDTH), layer), _resident((1, HEAD_DIM), layer), _resident((1, HEAD_DIM), layer)]
    args = [x, mod, w["in"], w["q_norm_g"], w["k_norm_g"]]
    if rope:
        in_specs += [pl.BlockSpec((tm, HEAD_DIM), lambda bi, i: (i, 0))] * 3
        args += list(rope_tabs)
    out_specs = [pl.BlockSpec((1, N_Q_HEADS, tm, HEAD_DIM), lambda bi, i: (bi, 0, i, 0)), tok(KV_WIDTH),
                 pl.BlockSpec((1, KV_WIDTH, tm), lambda bi, i: (bi, 0, i)),
                 tok(POOL_WIDTH), tok(SGU_WIDTH), tok(SGU_WIDTH), tok(N_BRANCHES * D_MODEL)]
    sds = lambda wd, dt: jax.ShapeDtypeStruct((b, s, wd), dt)
    out_shape = [jax.ShapeDtypeStruct((b, N_Q_HEADS, s, HEAD_DIM), BF16), sds(KV_WIDTH, BF16),
                 jax.ShapeDtypeStruct((b, KV_WIDTH, s), BF16),
                 sds(POOL_WIDTH, F32), sds(SGU_WIDTH, BF16), sds(SGU_WIDTH, BF16), sds(N_BRANCHES * D_MODEL, BF16)]
    if emit_cache:
        out_specs += [tok(KV_WIDTH), tok(KV_WIDTH)]
        out_shape += [sds(KV_WIDTH, F32), sds(KV_WIDTH, F32)]
    return pl.pallas_call(
        functools.partial(_inproj_kernel, rope, emit_cache),
        grid=(b, s // tm), in_specs=in_specs, out_specs=out_specs, out_shape=out_shape,
        compiler_params=_params(2), name="inproj_rope" if rope else "inproj_ctx",
    )(*args)


def _attn_kernel(n_ctx, n_chunks, tkc, lookahead, q_ref, *refs):
    refs = list(refs)
    qn_ref = refs.pop(0) if lookahead else None
    k_ref, vt_ref = refs[:2]
    refs = refs[2:]
    if n_ctx:
        ck_ref, cv_ref = refs[:2]
        refs = refs[2:]
    o_ref, kall, vtall, st_scr, cm_scr, m_scr, l_scr, acc_scr = refs
    tq = q_ref.shape[2]
    n_units = m_scr.shape[0]
    hpu = N_Q_HEADS // n_units

    def scores(src_ref, c, slot, u):
        g = u * hpu // Q_PER_KV
        kc = kall[pl.ds(pl.multiple_of(c * tkc, tkc), tkc), g * HEAD_DIM:(g + 1) * HEAD_DIM]
        qu = src_ref[0, u * hpu:(u + 1) * hpu].reshape(hpu * tq, HEAD_DIM)
        st = lax.dot_general(kc, qu, (((1,), (1,)), ((), ())), preferred_element_type=F32)
        st_scr[slot, u] = st
        cm_scr[slot, u] = jnp.max(st, axis=0, keepdims=True)

    def accumulate(c, slot, u):
        g = u * hpu // Q_PER_KV
        m_old = m_scr[u]
        m_new = jnp.maximum(m_old, cm_scr[slot, u])
        p = jnp.exp2(st_scr[slot, u] - m_new)
        alpha = jnp.exp2(m_old - m_new)
        l_scr[u] = alpha * l_scr[u] + jnp.sum(p, axis=0, keepdims=True)
        vtc = vtall[c, g * HEAD_DIM:(g + 1) * HEAD_DIM, :]
        acc_scr[u] = alpha * acc_scr[u] + _dot(vtc, p.astype(BF16))
        m_scr[u] = m_new

    def finalize(u):
        ot = acc_scr[u] / l_scr[u]
        for j in range(hpu):
            hd = u * hpu + j
            o_ref[0, :, hd * HEAD_DIM:(hd + 1) * HEAD_DIM] = ot[:, j * tq:(j + 1) * tq].T.astype(BF16)

    def first_scores():
        for u in range(n_units):
            scores(q_ref, 0, 0, u)

    @pl.when(pl.program_id(1) == 0)
    def _stage_keys():
        if n_ctx:
            for g in range(N_KV_HEADS):
                kall[0:n_ctx, g * HEAD_DIM:(g + 1) * HEAD_DIM] = ck_ref[0, :, g, :].astype(BF16)
            cvt = jnp.concatenate([cv_ref[0, :, g, :].T for g in range(N_KV_HEADS)], axis=0).astype(BF16)
        kall[n_ctx:, :] = k_ref[0]
        for c in range(n_chunks):
            lo, hi = c * tkc, (c + 1) * tkc
            if lo < n_ctx:
                end = min(hi, n_ctx)
                vtall[c, :, 0:end - lo] = cvt[:, lo:end]
            if hi > n_ctx:
                beg = max(lo, n_ctx)
                vtall[c, :, beg - lo:tkc] = vt_ref[0, :, beg - n_ctx:hi - n_ctx]
        if lookahead:
            first_scores()

    if not lookahead:
        first_scores()
    m_scr[...] = jnp.full(m_scr.shape, -jnp.inf, F32)
    l_scr[...] = jnp.zeros(l_scr.shape, F32)
    acc_scr[...] = jnp.zeros(acc_scr.shape, F32)

    def step(c, slot, last):
        for u in range(n_units):
            if not last:
                scores(q_ref, c + 1, 1 - slot, u)
            elif lookahead:
                scores(qn_ref, 0, 1 - slot, u)
            accumulate(c, slot, u)
            if last:
                finalize(u)

    n_tail = 2 - n_chunks % 2
    n_pairs = (n_chunks - n_tail) // 2

    def pair_step(j, carry):
        step(2 * j, 0, False)
        step(2 * j + 1, 1, False)
        return carry

    lax.fori_loop(0, n_pairs, pair_step, 0)
    for c in range(2 * n_pairs, n_chunks):
        step(c, c % 2, c == n_chunks - 1)


def _attention(q, k, vt, ctx_k, ctx_v, layer, tq, tkc):
    b, _, s, _ = q.shape
    n_ctx = 0 if ctx_k is None else ctx_k.shape[2]
    total = n_ctx + s
    n_chunks = total // tkc
    n_q = s // tq
    lookahead = n_q > 1 and n_chunks % 2 == 0
    q_spec = lambda imap: pl.BlockSpec((1, N_Q_HEADS, tq, HEAD_DIM), imap)
    in_specs = [q_spec(lambda bi, i: (bi, 0, i, 0))]
    args = [q]
    if lookahead:
        in_specs.append(q_spec(lambda bi, i: (bi, 0, jnp.minimum(i + 1, n_q - 1), 0)))
        args.append(q)
    in_specs += [pl.BlockSpec((1, s, KV_WIDTH), lambda bi, i: (bi, 0, 0)),
                 pl.BlockSpec((1, KV_WIDTH, s), lambda bi, i: (bi, 0, 0))]
    args += [k, vt]
    if n_ctx:
        in_specs += [pl.BlockSpec((1, None, n_ctx, N_KV_HEADS, HEAD_DIM), lambda bi, i: (bi, layer, 0, 0, 0))] * 2
        args += [ctx_k, ctx_v]
    n_units = N_Q_HEADS // ATTN_HEADS_PER_UNIT
    lanes = ATTN_HEADS_PER_UNIT * tq
    return pl.pallas_call(
        functools.partial(_attn_kernel, n_ctx, n_chunks, tkc, lookahead),
        grid=(b, n_q), in_specs=in_specs,
        out_specs=pl.BlockSpec((1, tq, ATTN_WIDTH), lambda bi, i: (bi, i, 0)),
        out_shape=jax.ShapeDtypeStruct((b, s, ATTN_WIDTH), BF16),
        scratch_shapes=[pltpu.VMEM((total, KV_WIDTH), BF16),
                        pltpu.VMEM((n_chunks, KV_WIDTH, tkc), BF16),
                        pltpu.VMEM((2, n_units, tkc, lanes), F32),
                        pltpu.VMEM((2, n_units, 1, lanes), F32),
                        pltpu.VMEM((n_units, 1, lanes), F32),
                        pltpu.VMEM((n_units, 1, lanes), F32),
                        pltpu.VMEM((n_units, HEAD_DIM, lanes), F32)],
        compiler_params=_params(2), name="attn_lat" if n_ctx else "attn_ctx",
    )(*args)


def _mix_kernel(x_ref, mod_ref, attn_ref, xp_ref, xprev_ref, xnext_ref, cnt_ref, u_ref, vn_ref, g_ref,
                wao_ref, wpo_ref, wso_ref, wout_ref, wpg_ref, psc_ref, ws_ref, bs_ref, lng_ref, lnb_ref,
                o_ref, pool_scr, sgu_scr):
    tm = x_ref.shape[1]
    i = pl.program_id(1)
    nt = pl.num_programs(1)
    xc = xp_ref[0]
    xe = jnp.concatenate([jnp.where(i > 0, xprev_ref[0], 0.0), xc, jnp.where(i < nt - 1, xnext_ref[0], 0.0)], axis=0)
    n_rows = tm + 2 * POOL_HALO
    up = lambda a, s: pltpu.roll(a, n_rows - s, 0)
    n_c = tm // CHUNK
    attn = attn_ref[0]
    cols = D_MODEL // N_POOL_GROUPS
    attn_parts = []
    for g, w in enumerate(POOL_WINDOWS):
        gl = slice(g * POOL_GROUP_DIM, (g + 1) * POOL_GROUP_DIM)
        acc = xe[:, gl]
        span = 1
        while span < w:
            acc = acc + up(acc, span)
            span *= 2
        first = POOL_HALO - w // 2
        sums = (up(acc, first) if first else acc)[0:tm]
        pooled = (sums / cnt_ref[:, gl] - xc[:, gl]).astype(BF16)
        pool_scr[:, gl] = (_dot(pooled, wpg_ref[g]) * psc_ref[:, gl]).astype(BF16)

        cs = slice(g * cols, (g + 1) * cols)
        attn_parts.append(g_ref[0, :, cs].astype(F32) * _dot(attn, wao_ref[:, cs]))

        vcat = jnp.concatenate([vn_ref[0, c * CHUNK:(c + 1) * CHUNK, gl] for c in range(n_c)], axis=1)
        mixed = _dot(ws_ref[g], vcat) + bs_ref[:, g:g + 1]
        for c in range(n_c):
            rows = slice(c * CHUNK, (c + 1) * CHUNK)
            sgu_scr[rows, gl] = (u_ref[0, rows, gl].astype(F32) * mixed[:, c * CHUNK:(c + 1) * CHUNK]).astype(BF16)

    merged = jnp.concatenate(attn_parts, axis=1)
    merged = merged + g_ref[0, :, D_MODEL:2 * D_MODEL].astype(F32) * _dot(pool_scr[...], wpo_ref[...])
    merged = merged + g_ref[0, :, 2 * D_MODEL:].astype(F32) * _dot(sgu_scr[...], wso_ref[...])
    merged = merged.astype(BF16)
    g1 = mod_ref[:, 2 * D_MODEL:3 * D_MODEL]
    rb = tm // TAIL_ROW_BLOCKS
    for r in range(TAIL_ROW_BLOCKS):
        rows = slice(r * rb, (r + 1) * rb)
        mix = _dot(merged[rows], wout_ref[...])
        o_ref[0, rows, :] = _layer_norm(DEEPNORM_ALPHA * x_ref[0, rows, :] + g1 * mix, lng_ref[...], lnb_ref[...])


def _pool_counts(seq):
    t = np.arange(seq, dtype=np.int32)[:, None]
    w = np.repeat(np.array(POOL_WINDOWS, dtype=np.int32), POOL_GROUP_DIM)[None, :]
    return jnp.asarray((np.minimum(t + (w - w // 2), seq) - np.maximum(t - w // 2, 0)).astype(np.float32))


def _mix(x, mod, mod_row, attn, xp, u, vn, gates, w, layer, tm):
    b, s, _ = x.shape
    hb = tm // POOL_HALO
    last_halo = s // POOL_HALO - 1
    tok = lambda wd: pl.BlockSpec((1, tm, wd), lambda bi, i: (bi, i, 0))
    in_specs = [tok(D_MODEL),
                _mod_spec(layer, mod_row),
                tok(ATTN_WIDTH), tok(POOL_WIDTH),
                pl.BlockSpec((1, POOL_HALO, POOL_WIDTH), lambda bi, i: (bi, jnp.maximum(i * hb - 1, 0), 0)),
                pl.BlockSpec((1, POOL_HALO, POOL_WIDTH), lambda bi, i: (bi, jnp.minimum((i + 1) * hb, last_halo), 0)),
                pl.BlockSpec((tm, POOL_WIDTH), lambda bi, i: (i, 0)),
                tok(SGU_WIDTH), tok(SGU_WIDTH), tok(N_BRANCHES * D_MODEL),
                *[_resident(shape, layer) for shape in (
                    (ATTN_WIDTH, D_MODEL), (POOL_WIDTH, D_MODEL), (SGU_WIDTH, D_MODEL), (D_MODEL, D_MODEL),
                    (N_POOL_GROUPS, POOL_GROUP_DIM, POOL_GROUP_DIM), (1, POOL_WIDTH), (N_SGU_GROUPS, CHUNK, CHUNK),
                    (CHUNK, N_SGU_GROUPS), (1, D_MODEL), (1, D_MODEL))]]
    return pl.pallas_call(
        _mix_kernel,
        grid=(b, s // tm), in_specs=in_specs, out_specs=tok(D_MODEL),
        out_shape=jax.ShapeDtypeStruct((b, s, D_MODEL), F32),
        scratch_shapes=[pltpu.VMEM((tm, POOL_WIDTH), BF16), pltpu.VMEM((tm, SGU_WIDTH), BF16)],
        compiler_params=_params(2), name="mix",
    )(x, mod, attn, xp, xp, xp, _pool_counts(s), u, vn, gates, w["attn_o"], w["pool_o"], w["sgu_o"], w["out"], w["pool_g"],
      w["pool_scale"], w["sgu"], w["b_sgu"], w["ln1_g"], w["ln1_b"])


def _ffn_kernel(x_ref, mod_ref, win_ref, wo_ref, lng_ref, lnb_ref, o_ref):
    x = x_ref[0]
    tm = x.shape[0]
    mod = mod_ref[...]
    sh2 = mod[:, 3 * D_MODEL:4 * D_MODEL]
    sc2 = mod[:, 4 * D_MODEL:5 * D_MODEL]
    g2 = mod[:, 5 * D_MODEL:6 * D_MODEL]
    h = (x * (1.0 + sc2) + sh2).astype(BF16)

    def hidden(lo, hi):
        a = _dot(h, win_ref[:, lo:hi])
        bgate = _dot(h, win_ref[:, D_FF + lo:D_FF + hi])
        return (a * _sigmoid(a) * bgate).astype(BF16)

    edges = list(zip(FFN_CHUNK_EDGES[:-1], FFN_CHUNK_EDGES[1:]))
    f = None
    for lo, hi in edges[:-1]:
        part = _dot(hidden(lo, hi), wo_ref[lo:hi, :])
        f = part if f is None else f + part
    lo, hi = edges[-1]
    act = hidden(lo, hi)
    rb = tm // TAIL_ROW_BLOCKS
    for r in range(TAIL_ROW_BLOCKS):
        rows = slice(r * rb, (r + 1) * rb)
        part = _dot(act[rows], wo_ref[lo:hi, :])
        fr = part if f is None else f[rows] + part
        o_ref[0, rows, :] = _layer_norm(DEEPNORM_ALPHA * x[rows] + g2 * fr, lng_ref[...], lnb_ref[...])


def _ffn(x, mod, mod_row, w, layer, tm):
    b, s, _ = x.shape
    tok = pl.BlockSpec((1, tm, D_MODEL), lambda bi, i: (bi, i, 0))
    in_specs = [tok,
                _mod_spec(layer, mod_row),
                _resident((D_MODEL, 2 * D_FF), layer), _resident((D_FF, D_MODEL), layer),
                _resident((1, D_MODEL), layer), _resident((1, D_MODEL), layer)]
    return pl.pallas_call(
        _ffn_kernel, grid=(b, s // tm), in_specs=in_specs, out_specs=tok,
        out_shape=jax.ShapeDtypeStruct((b, s, D_MODEL), F32),
        compiler_params=_params(2), name="ffn",
    )(x, mod, w["ffn_in"], w["ffn_o"], w["ln2_g"], w["ln2_b"])


def _rope_tables(seq):
    rows = seq // GRID_W
    row = jnp.repeat(jnp.arange(rows), GRID_W).astype(F32)
    col = jnp.tile(jnp.arange(GRID_W), rows).astype(F32)
    inv_freq = ROPE_THETA ** (-jnp.arange(ROPE_PAIRS, dtype=F32) / ROPE_PAIRS)
    ar = row[:, None] * inv_freq
    ac = col[:, None] * inv_freq
    z = jnp.zeros_like(ar)
    cos = jnp.concatenate([jnp.cos(ar), jnp.cos(ar), jnp.cos(ac), jnp.cos(ac)], axis=1)
    sina = jnp.concatenate([-jnp.sin(ar), z, -jnp.sin(ac), z], axis=1)
    sinb = jnp.concatenate([z, jnp.sin(ar), z, jnp.sin(ac)], axis=1)
    return cos, sina, sinb


def _trunk_layer(x, mod, mod_row, rope_tabs, ctx_k, ctx_v, w, layer, tm, emit_cache):
    outs = _inproj(x, mod, mod_row, w, layer, rope_tabs, tm, emit_cache)
    q, k, vt, xp, u, vn, gates = outs[:7]
    s = x.shape[1]
    attn = _attention(q, k, vt, ctx_k, ctx_v, layer, min(ATTN_Q_TILE, s), min(ATTN_KV_CHUNK, s))
    x = _mix(x, mod, mod_row, attn, xp, u, vn, gates, w, layer, min(MIX_TOKEN_TILE, s))
    if mod_row is None:
        x = _ffn(x, mod, mod_row, w, layer, min(FFN_TOKEN_TILE, s))
    else:
        b = x.shape[0]
        x = _ffn(x.reshape(1, b * s, D_MODEL), mod, mod_row, w, layer, min(FFN_TOKEN_TILE, b * s)).reshape(b, s, D_MODEL)
    return x, outs[7:]


def kernel(x_prompt, x_sample, cache_k, cache_v, c, c_ctx, w_ada, b_ada, w_in, q_norm_g, k_norm_g, w_pool_g,
           pool_scale, w_sgu, b_sgu, w_attn_o, w_pool_o, w_sgu_o, w_out, ln1_g, ln1_b, w_ffn_in, w_ffn_out,
           ln2_g, ln2_b):
    n_dec = c.shape[0]
    cvec = jnp.concatenate([c, c_ctx[None, :], jnp.zeros((ADA_ROWS - n_dec - 1, D_MODEL), F32)], axis=0)
    mods = _ada_rows(cvec, w_ada, b_ada).reshape(DEPTH, ADA_ROWS, 1, 6 * D_MODEL)
    rope_tabs = _rope_tables(x_sample.shape[1])
    rows = lambda a: a.reshape(DEPTH, 1, -1)
    w = {
        "in": w_in.astype(BF16), "q_norm_g": rows(q_norm_g), "k_norm_g": rows(k_norm_g),
        "pool_g": w_pool_g.astype(BF16), "pool_scale": rows(pool_scale),
        "sgu": w_sgu.astype(BF16), "b_sgu": jnp.swapaxes(b_sgu, 1, 2),
        "attn_o": w_attn_o.astype(BF16), "pool_o": w_pool_o.astype(BF16),
        "sgu_o": w_sgu_o.astype(BF16), "out": w_out.astype(BF16),
        "ln1_g": rows(ln1_g), "ln1_b": rows(ln1_b),
        "ffn_in": w_ffn_in.astype(BF16), "ffn_o": w_ffn_out.astype(BF16),
        "ln2_g": rows(ln2_g), "ln2_b": rows(ln2_b),
    }
    y_p, y_s = x_prompt, x_sample
    new_k, new_v = [], []
    for l in range(DEPTH):
        y_p, (k_ctx, v_ctx) = _trunk_layer(y_p, mods, n_dec, None, None, None, w, l, x_prompt.shape[1], True)
        new_k.append(k_ctx)
        new_v.append(v_ctx)
        y_s, _ = _trunk_layer(y_s, mods, None, rope_tabs, cache_k, cache_v, w, l, TOKEN_TILE, False)
    cache_shape = (x_prompt.shape[0], DEPTH, x_prompt.shape[1], N_KV_HEADS, HEAD_DIM)
    new_cache_k = jnp.stack(new_k, axis=1).reshape(cache_shape)
    new_cache_v = jnp.stack(new_v, axis=1).reshape(cache_shape)
    return (y_p, y_s, new_cache_k, new_cache_v)
```
